```python
import math
import jax
import jax.numpy as jnp
from jax import lax
import numpy as np

D_MODEL = 2048
BATCH = 8
SEQ = 2048
DEPTH = 2
DEC_BATCH = 128
DEC_SEQ = 8
PAST_LEN = 2048
PAGE_SIZE = 128

MIX_UNIT = D_MODEL // 8
GDN_W = 3 * MIX_UNIT
SSM_W = 2 * MIX_UNIT
SWA_W = 3 * MIX_UNIT
GDN_DK = 128
GDN_DV = 128
GDN_HEADS = GDN_W // GDN_DV
GDN_CONV = 4
GDN_CHUNK = 64
SSM_HEADDIM = 64
SSM_HEADS = SSM_W // SSM_HEADDIM
SSM_GROUPS = 2
SSM_STATE = 128
SSM_CONV = 4
SSM_CHUNK = 64
SSM_CONV_CH = SSM_W + 2 * SSM_GROUPS * SSM_STATE
SWA_HD = 128
SWA_HEADS = SWA_W // SWA_HD
DILATIONS = ((128, 1), (512, 4), (2048, 16))
W_MAX = 2048
Q_BLOCK = 128
ROPE_THETA = 10000.0
D_FF = 11 * D_MODEL // 4
FFN_CONV = 3
NORM_EPS = 1e-6
IN_SPLITS = (GDN_W, GDN_W, GDN_W, GDN_W, GDN_HEADS, GDN_HEADS,
             SSM_W, SSM_CONV_CH, SSM_HEADS,
             SWA_W, SWA_W, SWA_W)
IN_COLS = sum(IN_SPLITS)

kernel_name = 'hymba_gdn_ssd_dilated_convffn_step'


def split_cols(t, sizes):
    cuts = np.cumsum(np.array(sizes))[:-1].tolist()
    return jnp.split(t, cuts, axis=-1)


def rmsnorm(x, g):
    xf = x.astype(jnp.float32)
    y = xf * lax.rsqrt(jnp.mean(xf * xf, axis=-1, keepdims=True) + NORM_EPS)
    return (y * g.astype(jnp.float32)).astype(x.dtype)


def l2norm(x):
    xf = x.astype(jnp.float32)
    return xf * lax.rsqrt(jnp.sum(xf * xf, axis=-1, keepdims=True) + NORM_EPS)


def causal_dwconv(x, buf, w):
    t = x.shape[1]
    width = w.shape[0]
    xe = jnp.concatenate([buf.astype(x.dtype), x], axis=1)
    y = xe[:, 0:t] * w[0]
    for j in range(1, width):
        y = y + xe[:, j:j + t] * w[j]
    return y, xe[:, t:]


def rope(x, pos):
    half = x.shape[-1] // 2
    inv = ROPE_THETA ** (-jnp.arange(half, dtype=jnp.float32) / half)
    ang = pos.astype(jnp.float32)[:, None] * inv[None, :]
    cos = jnp.cos(ang)[None, :, None, :]
    sin = jnp.sin(ang)[None, :, None, :]
    xf = x.astype(jnp.float32)
    x1, x2 = xf[..., :half], xf[..., half:]
    return jnp.concatenate([x1 * cos - x2 * sin, x2 * cos + x1 * sin], axis=-1).astype(x.dtype)


def to_chunks(x, c):
    b, t, h = x.shape[:3]
    return jnp.swapaxes(x.reshape(b, t // c, c, h, *x.shape[3:]), 2, 3)


def from_chunks(x):
    x = jnp.swapaxes(x, 2, 3)
    return x.reshape(x.shape[0], -1, *x.shape[3:])


def gated_delta_chunked(q, k, v, beta, g, s0):
    t = q.shape[1]
    c = math.gcd(t, GDN_CHUNK)
    qc, kc, vc = to_chunks(q, c), to_chunks(k, c), to_chunks(v, c)
    bc, gc = to_chunks(beta, c), to_chunks(g, c)
    cg = jnp.cumsum(gc, axis=-1)
    causal = jnp.tril(jnp.ones((c, c), dtype=bool))
    strict = jnp.tril(jnp.ones((c, c), dtype=bool), -1)
    gam = jnp.exp(jnp.where(causal, cg[..., :, None] - cg[..., None, :], -jnp.inf))
    kb = kc * bc[..., None]
    amat = jnp.where(strict, jnp.einsum('bzhik,bzhjk->bzhij', kb, kc) * gam, 0.0)
    unit_lower = amat + jnp.eye(c, dtype=amat.dtype)
    rhs = jnp.concatenate([vc * bc[..., None], kb * jnp.exp(cg)[..., None]], axis=-1)
    sol = lax.linalg.triangular_solve(unit_lower, rhs, left_side=True, lower=True, unit_diagonal=True)
    u, w = sol[..., :GDN_DV], sol[..., GDN_DV:]
    qk = jnp.einsum('bzhik,bzhjk->bzhij', qc, kc) * gam
    qg = qc * jnp.exp(cg)[..., None]
    kd = kc * jnp.exp(cg[..., -1:] - cg)[..., None]
    gl = jnp.exp(cg[..., -1])

    def step(s, xs):
        u_, w_, qk_, qg_, kd_, gl_ = xs
        vnew = u_ - jnp.einsum('bhik,bhkv->bhiv', w_, s)
        o = jnp.einsum('bhik,bhkv->bhiv', qg_, s) + jnp.einsum('bhij,bhjv->bhiv', qk_, vnew)
        s = s * gl_[..., None, None] + jnp.einsum('bhjk,bhjv->bhkv', kd_, vnew)
        return s, o

    xs = tuple(jnp.moveaxis(arr, 1, 0) for arr in (u, w, qk, qg, kd, gl))
    s_fin, o = lax.scan(step, s0, xs)
    return from_chunks(jnp.moveaxis(o, 0, 1)), s_fin


def gdn_mixer(q, k, v, z, b, a, conv_buf, s0, conv_w, a_log, dt_bias, norm_g):
    bsz, t, _ = q.shape
    qkv, new_buf = causal_dwconv(jnp.concatenate([q, k, v], axis=-1), conv_buf, conv_w)
    qkv = jax.nn.silu(qkv)
    q, k, v = jnp.split(qkv, 3, axis=-1)
    q = l2norm(q.reshape(bsz, t, GDN_HEADS, GDN_DK)) * (GDN_DK ** -0.5)
    k = l2norm(k.reshape(bsz, t, GDN_HEADS, GDN_DK))
    v = v.reshape(bsz, t, GDN_HEADS, GDN_DV).astype(jnp.float32)
    beta = jax.nn.sigmoid(b.astype(jnp.float32))
    g = -jnp.exp(a_log.astype(jnp.float32)) * jax.nn.softplus(a.astype(jnp.float32) + dt_bias.astype(jnp.float32))
    o, s_fin = gated_delta_chunked(q, k, v, beta, g, s0.astype(jnp.float32))
    o = rmsnorm(o, norm_g) * jax.nn.silu(z.reshape(bsz, t, GDN_HEADS, GDN_DV).astype(jnp.float32))
    return o.reshape(bsz, t, GDN_W).astype(z.dtype), new_buf, s_fin


def ssd_chunked(x, dt, a_neg, bm, cm, h0):
    t = x.shape[1]
    c = math.gcd(t, SSM_CHUNK)
    rep = SSM_HEADS // SSM_GROUPS
    bh = jnp.repeat(bm, rep, axis=2)
    ch = jnp.repeat(cm, rep, axis=2)
    xc = to_chunks(x * dt[..., None], c)
    bc, cc = to_chunks(bh, c), to_chunks(ch, c)
    cg = jnp.cumsum(to_chunks(dt * a_neg, c), axis=-1)
    causal = jnp.tril(jnp.ones((c, c), dtype=bool))
    lmat = jnp.exp(jnp.where(causal, cg[..., :, None] - cg[..., None, :], -jnp.inf))
    y_intra = jnp.einsum('bzhij,bzhjp->bzhip', jnp.einsum('bzhis,bzhjs->bzhij', cc, bc) * lmat, xc)
    cgd = cc * jnp.exp(cg)[..., None]
    bd = bc * jnp.exp(cg[..., -1:] - cg)[..., None]
    gl = jnp.exp(cg[..., -1])

    def step(h, xs):
        cg_, bd_, x_, gl_ = xs
        y = jnp.einsum('bhis,bhsp->bhip', cg_, h)
        h = h * gl_[..., None, None] + jnp.einsum('bhjs,bhjp->bhsp', bd_, x_)
        return h, y

    xs = tuple(jnp.moveaxis(arr, 1, 0) for arr in (cgd, bd, xc, gl))
    h_fin, y_inter = lax.scan(step, h0, xs)
    return from_chunks(y_intra + jnp.moveaxis(y_inter, 0, 1)), h_fin


def ssd_mixer(z, xbc, dt, conv_buf, h0, conv_w, conv_b, dt_bias, a_log, d_skip, norm_g):
    bsz, t, _ = z.shape
    xbc, new_buf = causal_dwconv(xbc, conv_buf, conv_w)
    xbc = jax.nn.silu(xbc + conv_b)
    xs, bm, cm = split_cols(xbc, (SSM_W, SSM_GROUPS * SSM_STATE, SSM_GROUPS * SSM_STATE))
    xs = xs.reshape(bsz, t, SSM_HEADS, SSM_HEADDIM).astype(jnp.float32)
    bm = bm.reshape(bsz, t, SSM_GROUPS, SSM_STATE).astype(jnp.float32)
    cm = cm.reshape(bsz, t, SSM_GROUPS, SSM_STATE).astype(jnp.float32)
    dt = jax.nn.softplus(dt.astype(jnp.float32) + dt_bias.astype(jnp.float32))
    a_neg = -jnp.exp(a_log.astype(jnp.float32))
    y, h_fin = ssd_chunked(xs, dt, a_neg, bm, cm, h0.astype(jnp.float32))
    y = y + d_skip.astype(jnp.float32)[:, None] * xs
    y = y.reshape(bsz, t, SSM_W) * jax.nn.silu(z.astype(jnp.float32))
    y = rmsnorm(y.reshape(bsz, t, SSM_GROUPS, SSM_W // SSM_GROUPS), norm_g.reshape(SSM_GROUPS, -1))
    return y.reshape(bsz, t, SSM_W).astype(z.dtype), new_buf, h_fin


def dilated_attention(q, k_all, v_all, past):
    bsz, t, nh, hd = q.shape
    qb = math.gcd(t, Q_BLOCK)
    nb = t // qb
    q_blocks = jnp.swapaxes(q.reshape(bsz, nb, qb, nh, hd), 0, 1)
    starts = past + jnp.arange(nb, dtype=jnp.int32) * qb
    scale = hd ** -0.5

    def one_block(args):
        qblk, s = args
        nums, dens, maxes = [], [], []
        for window, dil in DILATIONS:
            n_res = min(dil, qb)
            n_q = qb // n_res
            n_j = window // dil
            n_k = n_j + n_q
            r = jnp.arange(n_res, dtype=jnp.int32)
            kk = jnp.arange(n_k, dtype=jnp.int32)
            m = jnp.arange(n_q, dtype=jnp.int32)
            idx = s + r[:, None] - window + dil * kk[None, :]
            safe = jnp.maximum(idx, 0)
            kg = k_all[:, safe]
            vg = v_all[:, safe].astype(jnp.float32)
            qr = qblk.reshape(bsz, n_q, n_res, nh, hd)
            sc = jnp.einsum('bmrhd,brkhd->bhrmk', qr, kg, preferred_element_type=jnp.float32) * scale
            band = (kk[None, :] >= m[:, None]) & (kk[None, :] <= m[:, None] + n_j)
            mask = band[None, :, :] & (idx >= 0)[:, None, :]
            sc = jnp.where(mask, sc, -jnp.inf)
            mx = jnp.max(sc, axis=-1)
            p = jnp.exp(sc - mx[..., None])
            den = jnp.sum(p, axis=-1)
            num = jnp.einsum('bhrmk,brkhd->bmrhd', p, vg)
            nums.append(num.reshape(bsz, qb, nh, hd))
            dens.append(jnp.transpose(den, (0, 3, 2, 1)).reshape(bsz, qb, nh))
            maxes.append(jnp.transpose(mx, (0, 3, 2, 1)).reshape(bsz, qb, nh))
        mx_all = jnp.stack(maxes)
        wts = jnp.exp(mx_all - jnp.max(mx_all, axis=0, keepdims=True))
        num = jnp.sum(jnp.stack(nums) * wts[..., None], axis=0)
        den = jnp.sum(jnp.stack(dens) * wts, axis=0)
        return num / den[..., None]

    out = lax.map(one_block, (q_blocks, starts))
    return jnp.swapaxes(out, 0, 1).reshape(bsz, t, nh, hd).astype(q.dtype)


def decoder_layer(x, c, pos0, gdn_conv_buf, gdn_s, ssm_conv_buf, ssm_h, k_past, v_past, ffn_conv_buf,
                  w_ada, b_ada, g_pre_mix, g_post_mix, g_pre_ffn, g_post_ffn, w_in,
                  gdn_conv_w, gdn_a_log, gdn_dt_bias, gdn_norm_g,
                  ssm_conv_w, ssm_conv_b, ssm_dt_bias, ssm_a_log, ssm_d, ssm_norm_g,
                  w_out, w_up, ffn_conv_w, ffn_conv_b, w_down):
    bsz, t, _ = x.shape
    mod = jax.nn.silu(c) @ w_ada + b_ada
    shift1, scale1, gate1, shift2, scale2, gate2 = [mm[:, None, :] for mm in jnp.split(mod, 6, axis=-1)]
    h = rmsnorm(x, g_pre_mix) * (1 + scale1) + shift1
    gq, gk, gv, gz, gb, ga, sz, sxbc, sdt, aq, ak, av = split_cols(h @ w_in, IN_SPLITS)
    o_a, new_gdn_conv, new_gdn_s = gdn_mixer(gq, gk, gv, gz, gb, ga, gdn_conv_buf, gdn_s,
                                             gdn_conv_w, gdn_a_log, gdn_dt_bias, gdn_norm_g)
    o_b, new_ssm_conv, new_ssm_h = ssd_mixer(sz, sxbc, sdt, ssm_conv_buf, ssm_h, ssm_conv_w, ssm_conv_b,
                                             ssm_dt_bias, ssm_a_log, ssm_d, ssm_norm_g)
    pos = pos0 + jnp.arange(t, dtype=jnp.int32)
    q = rope(aq.reshape(bsz, t, SWA_HEADS, SWA_HD), pos)
    k = rope(ak.reshape(bsz, t, SWA_HEADS, SWA_HD), pos)
    v = av.reshape(bsz, t, SWA_HEADS, SWA_HD)
    k_all = jnp.concatenate([k_past.astype(k.dtype), k], axis=1)
    v_all = jnp.concatenate([v_past.astype(v.dtype), v], axis=1)
    o_c = dilated_attention(q, k_all, v_all, k_past.shape[1])
    mix = jnp.concatenate([o_a, o_b, o_c.reshape(bsz, t, SWA_W)], axis=-1) @ w_out
    x = x + gate1 * rmsnorm(mix, g_post_mix)
    h = rmsnorm(x, g_pre_ffn) * (1 + scale2) + shift2
    up, new_ffn_conv = causal_dwconv(h @ w_up, ffn_conv_buf, ffn_conv_w)
    gt, u = jnp.split(up + ffn_conv_b, 2, axis=-1)
    x = x + gate2 * rmsnorm((jax.nn.silu(gt) * u) @ w_down, g_post_ffn)
    keep = min(W_MAX, t)
    return (x, new_gdn_conv, new_gdn_s, new_ssm_conv, new_ssm_h, k[:, t - keep:], v[:, t - keep:], new_ffn_conv)


def setup_inputs(seed: int = 0) -> dict:
    key = jax.random.key(seed)
    keys = jax.random.split(key, 48)
    counter = [0]

    def nxt():
        kk = keys[counter[0]]
        counter[0] += 1
        return kk

    def nrm(shape, scale):
        return jax.random.normal(nxt(), shape, jnp.float32) * scale

    def gain(shape):
        return 1.0 + nrm(shape, 0.02)

    def dt_bias_init(shape):
        u = jax.random.uniform(nxt(), shape, jnp.float32)
        dt = jnp.exp(u * (math.log(0.1) - math.log(0.001)) + math.log(0.001))
        return dt + jnp.log(-jnp.expm1(-dt))

    def a_log_init(shape):
        return jnp.log(jax.random.uniform(nxt(), shape, jnp.float32, 1.0, 16.0))

    l_cache = min(W_MAX, PAST_LEN)
    return {
        'x_prompt': nrm((BATCH, SEQ, D_MODEL), 1.0),
        'x_sample': nrm((DEC_BATCH, DEC_SEQ, D_MODEL), 1.0),
        'c_prompt': nrm((BATCH, D_MODEL), 1.0),
        'c_sample': nrm((DEC_BATCH, D_MODEL), 1.0),
        'state_gdn_conv': nrm((DEPTH, DEC_BATCH, GDN_CONV - 1, 3 * GDN_W), 1.0),
        'state_gdn': nrm((DEPTH, DEC_BATCH, GDN_HEADS, GDN_DK, GDN_DV), 0.1),
        'state_ssm_conv': nrm((DEPTH, DEC_BATCH, SSM_CONV - 1, SSM_CONV_CH), 1.0),
        'state_ssm': nrm((DEPTH, DEC_BATCH, SSM_HEADS, SSM_STATE, SSM_HEADDIM), 0.1),
        'cache_k': nrm((DEPTH, DEC_BATCH, l_cache, SWA_HEADS, SWA_HD), 1.0),
        'cache_v': nrm((DEPTH, DEC_BATCH, l_cache, SWA_HEADS, SWA_HD), 1.0),
        'state_ffn_conv': nrm((DEPTH, DEC_BATCH, FFN_CONV - 1, 2 * D_FF), 1.0),
        'w_ada': nrm((DEPTH, D_MODEL, 6 * D_MODEL), 0.5 * D_MODEL ** -0.5),
        'b_ada': nrm((DEPTH, 6 * D_MODEL), 0.02),
        'g_pre_mix': gain((DEPTH, D_MODEL)),
        'g_post_mix': gain((DEPTH, D_MODEL)),
        'g_pre_ffn': gain((DEPTH, D_MODEL)),
        'g_post_ffn': gain((DEPTH, D_MODEL)),
        'w_in': nrm((DEPTH, D_MODEL, IN_COLS), D_MODEL ** -0.5),
        'gdn_conv_w': nrm((DEPTH, GDN_CONV, 3 * GDN_W), GDN_CONV ** -0.5),
        'gdn_a_log': a_log_init((DEPTH, GDN_HEADS)),
        'gdn_dt_bias': dt_bias_init((DEPTH, GDN_HEADS)),
        'gdn_norm_g': gain((DEPTH, GDN_DV)),
        'ssm_conv_w': nrm((DEPTH, SSM_CONV, SSM_CONV_CH), SSM_CONV ** -0.5),
        'ssm_conv_b': nrm((DEPTH, SSM_CONV_CH), 0.02),
        'ssm_dt_bias': dt_bias_init((DEPTH, SSM_HEADS)),
        'ssm_a_log': a_log_init((DEPTH, SSM_HEADS)),
        'ssm_d': gain((DEPTH, SSM_HEADS)),
        'ssm_norm_g': gain((DEPTH, SSM_W)),
        'w_out': nrm((DEPTH, D_MODEL, D_MODEL), D_MODEL ** -0.5),
        'w_up': nrm((DEPTH, D_MODEL, 2 * D_FF), D_MODEL ** -0.5),
        'ffn_conv_w': nrm((DEPTH, FFN_CONV, 2 * D_FF), FFN_CONV ** -0.5),
        'ffn_conv_b': nrm((DEPTH, 2 * D_FF), 0.02),
        'w_down': nrm((DEPTH, D_FF, D_MODEL), D_FF ** -0.5),
    }


def reference(x_prompt, x_sample, c_prompt, c_sample, state_gdn_conv, state_gdn, state_ssm_conv, state_ssm,
              cache_k, cache_v, state_ffn_conv, w_ada, b_ada, g_pre_mix, g_post_mix, g_pre_ffn, g_post_ffn,
              w_in, gdn_conv_w, gdn_a_log, gdn_dt_bias, gdn_norm_g, ssm_conv_w, ssm_conv_b, ssm_dt_bias,
              ssm_a_log, ssm_d, ssm_norm_g, w_out, w_up, ffn_conv_w, ffn_conv_b, w_down):
    bp = x_prompt.shape[0]
    xdt = x_prompt.dtype
    p_init = (
        jnp.zeros((bp, GDN_CONV - 1, 3 * GDN_W), xdt),
        jnp.zeros((bp, GDN_HEADS, GDN_DK, GDN_DV), jnp.float32),
        jnp.zeros((bp, SSM_CONV - 1, SSM_CONV_CH), xdt),
        jnp.zeros((bp, SSM_HEADS, SSM_STATE, SSM_HEADDIM), jnp.float32),
        jnp.zeros((bp, 0, SWA_HEADS, SWA_HD), xdt),
        jnp.zeros((bp, 0, SWA_HEADS, SWA_HD), xdt),
        jnp.zeros((bp, FFN_CONV - 1, 2 * D_FF), xdt),
    )
    y_prompt, y_sample = x_prompt, x_sample
    p_rows, s_rows = [], []
    for layer in range(DEPTH):
        weights = (w_ada[layer], b_ada[layer], g_pre_mix[layer], g_post_mix[layer], g_pre_ffn[layer],
                   g_post_ffn[layer], w_in[layer], gdn_conv_w[layer], gdn_a_log[layer], gdn_dt_bias[layer],
                   gdn_norm_g[layer], ssm_conv_w[layer], ssm_conv_b[layer], ssm_dt_bias[layer],
                   ssm_a_log[layer], ssm_d[layer], ssm_norm_g[layer], w_out[layer], w_up[layer],
                   ffn_conv_w[layer], ffn_conv_b[layer], w_down[layer])
        out_p = decoder_layer(y_prompt, c_prompt, 0, *p_init, *weights)
        out_s = decoder_layer(y_sample, c_sample, PAST_LEN, state_gdn_conv[layer], state_gdn[layer],
                              state_ssm_conv[layer], state_ssm[layer], cache_k[layer], cache_v[layer],
                              state_ffn_conv[layer], *weights)
        y_prompt, y_sample = out_p[0], out_s[0]
        p_rows.append(out_p[1:])
        s_rows.append(out_s[1:])
    p_gdn_conv, p_gdn, p_ssm_conv, p_ssm, p_k, p_v, p_ffn_conv = [jnp.stack(a) for a in zip(*p_rows)]
    s_gdn_conv, s_gdn, s_ssm_conv, s_ssm, s_k, s_v, s_ffn_conv = [jnp.stack(a) for a in zip(*s_rows)]
    return (y_prompt, y_sample, p_gdn_conv, p_gdn, p_ssm_conv, p_ssm, p_k, p_v, p_ffn_conv,
            s_gdn_conv, s_gdn, s_ssm_conv, s_ssm, s_k, s_v, s_ffn_conv)
```

```python
import functools
import math

import jax
import jax.numpy as jnp
import numpy as np
from jax import lax
from jax.experimental import pallas as pl
from jax.experimental.pallas import tpu as pltpu

F32 = jnp.float32
BF16 = jnp.bfloat16

D_MODEL = 2048
MIX_UNIT = D_MODEL // 8
GDN_W = 3 * MIX_UNIT
SSM_W = 2 * MIX_UNIT
SWA_W = 3 * MIX_UNIT
GDN_DK = 128
GDN_DV = 128
GDN_HEADS = GDN_W // GDN_DV
GDN_CONV = 4
GDN_CHUNK = 64
SSM_HEADDIM = 64
SSM_HEADS = SSM_W // SSM_HEADDIM
SSM_GROUPS = 2
SSM_STATE = 128
SSM_CONV = 4
SSM_CHUNK = 64
SSM_CONV_CH = SSM_W + 2 * SSM_GROUPS * SSM_STATE
SWA_HD = 128
SWA_HEADS = SWA_W // SWA_HD
DILATIONS = ((128, 1), (512, 4), (2048, 16))
W_MAX = 2048
ROPE_THETA = 10000.0
D_FF = 11 * D_MODEL // 4
FFN_CONV = 3
NORM_EPS = 1e-6

LANES = 128
SUBLANES = 8
VMEM_LIMIT = 56 * 1024 * 1024

PK_GQ = 0
PK_GK = PK_GQ + GDN_W
PK_GV = PK_GK + GDN_W
PK_GZ = PK_GV + GDN_W
PK_SZ = PK_GZ + GDN_W
PK_SX = PK_SZ + SSM_W
PK_SBC = PK_SX + SSM_W
PK_AQ = PK_SX + SSM_CONV_CH
PK_AK = PK_AQ + SWA_W
PK_AV = PK_AK + SWA_W
PK_SM = PK_AV + SWA_W
PK_COLS = PK_SM + LANES
SM_BETA = 0
SM_A = GDN_HEADS
SM_DT = 2 * GDN_HEADS
NEG_BIG = -1e30


def _cparams(sem):
    return pltpu.CompilerParams(dimension_semantics=sem, vmem_limit_bytes=VMEM_LIMIT)


def _sigmoid(x):
    return 1.0 / (1.0 + jnp.exp(-x))


def _silu(x):
    return x * _sigmoid(x)


def _softplus(x):
    return jnp.maximum(x, 0.0) + jnp.log1p(jnp.exp(-jnp.abs(x)))


def _mm(a, b):
    return jnp.dot(a.astype(BF16), b.astype(BF16), preferred_element_type=F32)


def _mm_nt(a, b):
    return lax.dot_general(a.astype(BF16), b.astype(BF16), (((1,), (1,)), ((), ())), preferred_element_type=F32)


def _mm_hi(a, b):
    return jnp.dot(a, b, preferred_element_type=F32, precision=lax.Precision.HIGHEST)


def _rms(x, g):
    return x * lax.rsqrt(jnp.mean(x * x, axis=-1, keepdims=True) + NORM_EPS) * g


def _ada_kernel(c_ref, w_ref, b_ref, o_ref):
    a = _silu(c_ref[...]).astype(BF16)
    o_ref[...] = jnp.dot(a, w_ref[...].astype(BF16), preferred_element_type=F32) + b_ref[...]


def _ada(c_all, w_ada, b_ada):
    depth, d, n = w_ada.shape
    r = c_all.shape[0]
    tn = 1024
    return pl.pallas_call(
        _ada_kernel,
        grid=(depth, n // tn),
        in_specs=[pl.BlockSpec((r, d), lambda l, j: (0, 0)),
                  pl.BlockSpec((None, d, tn), lambda l, j: (l, 0, j)),
                  pl.BlockSpec((None, 1, tn), lambda l, j: (l, 0, j))],
        out_specs=pl.BlockSpec((None, r, tn), lambda l, j: (l, 0, j)),
        out_shape=jax.ShapeDtypeStruct((depth, r, n), F32),
        compiler_params=_cparams(("arbitrary", "arbitrary")),
        name="ada",
    )(c_all, w_ada, b_ada.reshape(depth, 1, n))


def _inproj_kernel(x_ref, sh_ref, sc_ref, g_ref, w_ref, o_ref, h_scr):
    @pl.when(pl.program_id(2) == 0)
    def _():
        h = _rms(x_ref[...], g_ref[...]) * (1.0 + sc_ref[...]) + sh_ref[...]
        h_scr[...] = h.reshape(h_scr.shape).astype(BF16)

    o_ref[...] = jnp.dot(h_scr[...], w_ref[...], preferred_element_type=F32)


def _inproj(x, mod, g, w, layer, bb, tt):
    b, t, d = x.shape
    n = w.shape[-1]
    tn = 1408
    nt = t // tt
    return pl.pallas_call(
        _inproj_kernel,
        grid=(b // bb, nt, n // tn),
        in_specs=[pl.BlockSpec((bb, tt, d), lambda bi, ti, j: (bi, ti, 0)),
                  pl.BlockSpec((None, bb, 1, d), lambda bi, ti, j: (layer, bi, 0, 0)),
                  pl.BlockSpec((None, bb, 1, d), lambda bi, ti, j: (layer, bi, 0, 1)),
                  pl.BlockSpec((None, 1, d), lambda bi, ti, j: (layer, 0, 0)),
                  pl.BlockSpec((None, d, tn), lambda bi, ti, j: (layer, 0, j))],
        out_specs=pl.BlockSpec((bb * tt, tn), lambda bi, ti, j: (bi * nt + ti, j)),
        out_shape=jax.ShapeDtypeStruct((b * t, n), F32),
        scratch_shapes=[pltpu.VMEM((bb * tt, d), BF16)],
        compiler_params=_cparams(("arbitrary", "arbitrary", "arbitrary")),
        name="inproj",
    )(x, mod, mod, g, w)


def _conv_from_scratch(xs, w_ref, width, t):
    lo = SUBLANES - (width - 1)
    y = xs[lo:lo + t, :] * w_ref[0:1, :]
    for j in range(1, width):
        y = y + xs[lo + j:lo + j + t, :] * w_ref[j:j + 1, :]
    return y


def _conv_stage(xs, x_new, hist_ref, first, width, t):
    lo = SUBLANES - (width - 1)

    @pl.when(first)
    def _():
        if hist_ref is None:
            xs[lo:SUBLANES, :] = jnp.zeros((width - 1, xs.shape[1]), F32)
        else:
            xs[lo:SUBLANES, :] = hist_ref[...]

    @pl.when(jnp.logical_not(first))
    def _():
        xs[lo:SUBLANES, :] = xs[t + lo:t + SUBLANES, :]

    xs[SUBLANES:SUBLANES + t, :] = x_new


def _tri_masks(c):
    row = lax.broadcasted_iota(jnp.int32, (c, c), 0)
    col = lax.broadcasted_iota(jnp.int32, (c, c), 1)
    return row == col, row >= col, row > col, row <= col


def _unit_lower_inverse(a, eye_f, c):
    x = eye_f - a
    p = _mm_hi(a, a)
    n = 2
    while True:
        x = x + _mm_hi(x, p)
        n *= 2
        if n >= c:
            return x
        p = _mm_hi(p, p)


def _gdn_kernel(*refs, c, nc, zero_init):
    if zero_init:
        (alog_ref, dtb_ref, q_ref, k_ref, v_ref, z_ref, sm_ref, wq_ref, wk_ref, wv_ref, ng_ref,
         o_ref, sfin_ref, xq, xk, xv, s_scr) = refs
        cq_ref = ck_ref = cv_ref = s0_ref = None
    else:
        (alog_ref, dtb_ref, q_ref, k_ref, v_ref, z_ref, sm_ref, wq_ref, wk_ref, wv_ref, ng_ref,
         cq_ref, ck_ref, cv_ref, s0_ref, o_ref, sfin_ref, xq, xk, xv, s_scr) = refs
    h = pl.program_id(1)
    i = pl.program_id(2)
    tb = c * nc
    first = i == 0

    @pl.when(first)
    def _():
        if zero_init:
            s_scr[...] = jnp.zeros_like(s_scr)
        else:
            s_scr[...] = s0_ref[...]

    _conv_stage(xq, q_ref[...], cq_ref, first, GDN_CONV, tb)
    _conv_stage(xk, k_ref[...], ck_ref, first, GDN_CONV, tb)
    _conv_stage(xv, v_ref[...], cv_ref, first, GDN_CONV, tb)
    q = _silu(_conv_from_scratch(xq, wq_ref, GDN_CONV, tb))
    k = _silu(_conv_from_scratch(xk, wk_ref, GDN_CONV, tb))
    v = _silu(_conv_from_scratch(xv, wv_ref, GDN_CONV, tb))
    q = q * lax.rsqrt(jnp.sum(q * q, axis=-1, keepdims=True) + NORM_EPS) * (GDN_DK ** -0.5)
    k = k * lax.rsqrt(jnp.sum(k * k, axis=-1, keepdims=True) + NORM_EPS)

    sm = sm_ref[...]
    lane = lax.broadcasted_iota(jnp.int32, sm.shape, 1)
    b_raw = jnp.sum(jnp.where(lane == SM_BETA + h, sm, 0.0), axis=1, keepdims=True)
    a_raw = jnp.sum(jnp.where(lane == SM_A + h, sm, 0.0), axis=1, keepdims=True)
    beta = _sigmoid(b_raw)
    a_neg = -jnp.exp(jnp.full((1, 1), alog_ref[h], F32))
    g = a_neg * _softplus(a_raw + dtb_ref[h])

    eye, causal, strict, upper = _tri_masks(c)
    eye_f = eye.astype(F32)
    ng = ng_ref[...]
    for ci in range(nc):
        sl = slice(ci * c, (ci + 1) * c)
        qc, kc, vc, bc, gc = q[sl], k[sl], v[sl], beta[sl], g[sl]
        g_row = jnp.sum(jnp.where(eye, gc, 0.0), axis=0, keepdims=True)
        cg_col = jnp.sum(jnp.where(causal, g_row, 0.0), axis=1, keepdims=True)
        cg_row = jnp.sum(jnp.where(upper, gc, 0.0), axis=0, keepdims=True)
        gam = jnp.where(causal, jnp.exp(jnp.where(causal, cg_col - cg_row, 0.0)), 0.0)
        kb = kc * bc
        amat = jnp.where(strict, _mm_nt(kb, kc) * gam, 0.0)
        tinv = _unit_lower_inverse(amat, eye_f, c)
        ecg = jnp.exp(cg_col)
        u = _mm_hi(tinv, vc * bc)
        w = _mm_hi(tinv, kb * ecg)
        qk = _mm_nt(qc, kc) * gam
        cg_last = cg_col[c - 1:c, :]
        kd = kc * jnp.exp(cg_last - cg_col)
        s = s_scr[...]
        vnew = u - _mm(w, s)
        o = _mm(qc * ecg, s) + _mm(qk, vnew)
        s_scr[...] = s * jnp.exp(cg_last) + _mm(kd.T, vnew)
        o = _rms(o, ng) * _silu(z_ref[sl, :])
        o_ref[sl, :] = o.astype(o_ref.dtype)

    sfin_ref[...] = s_scr[...]


def _gdn(proj, conv_buf, s0, conv_w, a_log, dt_bias, norm_g, layer, b, t):
    zero_init = s0 is None
    c = math.gcd(t, GDN_CHUNK)
    nc = max(1, min(4, t // c))
    tb = c * nc
    nt = t // tb
    hq, hk, hv, hz = PK_GQ // LANES, PK_GK // LANES, PK_GV // LANES, PK_GZ // LANES
    row = lambda bi, h, i: bi * nt + i
    smem = pl.BlockSpec(memory_space=pltpu.SMEM)
    in_specs = [smem, smem,
                pl.BlockSpec((tb, LANES), lambda bi, h, i: (row(bi, h, i), hq + h)),
                pl.BlockSpec((tb, LANES), lambda bi, h, i: (row(bi, h, i), hk + h)),
                pl.BlockSpec((tb, LANES), lambda bi, h, i: (row(bi, h, i), hv + h)),
                pl.BlockSpec((tb, LANES), lambda bi, h, i: (row(bi, h, i), hz + h)),
                pl.BlockSpec((tb, LANES), lambda bi, h, i: (row(bi, h, i), PK_SM // LANES)),
                pl.BlockSpec((None, GDN_CONV, LANES), lambda bi, h, i: (layer, 0, h)),
                pl.BlockSpec((None, GDN_CONV, LANES), lambda bi, h, i: (layer, 0, GDN_HEADS + h)),
                pl.BlockSpec((None, GDN_CONV, LANES), lambda bi, h, i: (layer, 0, 2 * GDN_HEADS + h)),
                pl.BlockSpec((None, 1, LANES), lambda bi, h, i: (layer, 0, 0))]
    args = [a_log[layer], dt_bias[layer], proj, proj, proj, proj, proj, conv_w, conv_w, conv_w,
            norm_g.reshape(norm_g.shape[0], 1, LANES)]
    if not zero_init:
        in_specs += [pl.BlockSpec((None, None, GDN_CONV - 1, LANES), lambda bi, h, i: (layer, bi, 0, h)),
                     pl.BlockSpec((None, None, GDN_CONV - 1, LANES), lambda bi, h, i: (layer, bi, 0, GDN_HEADS + h)),
                     pl.BlockSpec((None, None, GDN_CONV - 1, LANES),
                                  lambda bi, h, i: (layer, bi, 0, 2 * GDN_HEADS + h)),
                     pl.BlockSpec((None, None, None, GDN_DK, GDN_DV), lambda bi, h, i: (layer, bi, h, 0, 0))]
        args += [conv_buf, conv_buf, conv_buf, s0]
    o, s_fin = pl.pallas_call(
        functools.partial(_gdn_kernel, c=c, nc=nc, zero_init=zero_init),
        grid=(b, GDN_HEADS, nt),
        in_specs=in_specs,
        out_specs=[pl.BlockSpec((tb, LANES), lambda bi, h, i: (row(bi, h, i), h)),
                   pl.BlockSpec((None, None, GDN_DK, GDN_DV), lambda bi, h, i: (bi, h, 0, 0))],
        out_shape=[jax.ShapeDtypeStruct((b * t, GDN_W), BF16),
                   jax.ShapeDtypeStruct((b, GDN_HEADS, GDN_DK, GDN_DV), F32)],
        scratch_shapes=[pltpu.VMEM((tb + SUBLANES, LANES), F32)] * 3 + [pltpu.VMEM((GDN_DK, GDN_DV), F32)],
        compiler_params=_cparams(("arbitrary", "arbitrary", "arbitrary")),
        name="gdn",
    )(*args)
    return o, s_fin


def _ssd_kernel(*refs, c, nc, zero_init):
    if zero_init:
        (z_ref, x_ref, bc_ref, sm_ref, wx_ref, wbc_ref, bx_ref, bbc_ref, dtb_ref, alog_ref, dsk_ref, ng_ref,
         y_ref, hfin_ref, xs, xbc, h_scr) = refs
        cx_ref = cbc_ref = h0_ref = None
    else:
        (z_ref, x_ref, bc_ref, sm_ref, wx_ref, wbc_ref, bx_ref, bbc_ref, dtb_ref, alog_ref, dsk_ref, ng_ref,
         cx_ref, cbc_ref, h0_ref, y_ref, hfin_ref, xs, xbc, h_scr) = refs
    i = pl.program_id(1)
    tb = c * nc
    first = i == 0

    @pl.when(first)
    def _():
        if zero_init:
            h_scr[...] = jnp.zeros_like(h_scr)
        else:
            h_scr[...] = h0_ref[...]

    _conv_stage(xs, x_ref[...], cx_ref, first, SSM_CONV, tb)
    _conv_stage(xbc, bc_ref[...], cbc_ref, first, SSM_CONV, tb)
    xv = _silu(_conv_from_scratch(xs, wx_ref, SSM_CONV, tb) + bx_ref[...])
    bcv = _silu(_conv_from_scratch(xbc, wbc_ref, SSM_CONV, tb) + bbc_ref[...])
    gs = SSM_GROUPS * SSM_STATE

    dt_all = _softplus(sm_ref[...] + dtb_ref[...])
    da_all = dt_all * (-jnp.exp(alog_ref[...]))
    _, causal, _, _ = _tri_masks(c)
    tril_f = causal.astype(F32)
    rep = SSM_HEADS // SSM_GROUPS
    gw = SSM_W // SSM_GROUPS
    for ci in range(nc):
        sl = slice(ci * c, (ci + 1) * c)
        cg_all = _mm_hi(tril_f, da_all[sl])
        cg_t = cg_all.T
        ys = []
        for grp in range(SSM_GROUPS):
            bm = bcv[sl, grp * SSM_STATE:(grp + 1) * SSM_STATE]
            cm = bcv[sl, gs + grp * SSM_STATE:gs + (grp + 1) * SSM_STATE]
            cb = _mm_nt(cm, bm)
            for hh in range(rep):
                hd = grp * rep + hh
                ln = SM_DT + hd
                cg_col = cg_all[:, ln:ln + 1]
                cg_row = cg_t[ln:ln + 1, :]
                lmat = jnp.where(causal, jnp.exp(jnp.where(causal, cg_col - cg_row, 0.0)), 0.0)
                xh = xv[sl, hd * SSM_HEADDIM:(hd + 1) * SSM_HEADDIM]
                xdt = xh * dt_all[sl, ln:ln + 1]
                cg_last = cg_col[c - 1:c, :]
                hst = h_scr[hd]
                y = _mm(cb * lmat, xdt) + _mm(cm * jnp.exp(cg_col), hst)
                h_scr[hd] = hst * jnp.exp(cg_last) + _mm((bm * jnp.exp(cg_last - cg_col)).T, xdt)
                ys.append(y)
        yc = jnp.concatenate(ys, axis=-1) + dsk_ref[...] * xv[sl]
        yc = yc * _silu(z_ref[sl, :])
        ng = ng_ref[...]
        outs = [_rms(yc[:, gi * gw:(gi + 1) * gw], ng[:, gi * gw:(gi + 1) * gw]) for gi in range(SSM_GROUPS)]
        y_ref[sl, :] = jnp.concatenate(outs, axis=-1).astype(y_ref.dtype)

    hfin_ref[...] = h_scr[...]


def _lane_row(vals, offset):
    depth, n = vals.shape
    return jnp.pad(vals.astype(F32), ((0, 0), (offset, LANES - offset - n))).reshape(depth, 1, LANES)


def _ssd(proj, conv_buf, h0, conv_w, conv_b, dt_bias, a_log, d_skip, norm_g, layer, b, t):
    zero_init = h0 is None
    c = math.gcd(t, SSM_CHUNK)
    nc = max(1, min(4, t // c))
    tb = c * nc
    nt = t // tb
    depth = conv_w.shape[0]
    row = lambda bi, i: bi * nt + i
    wblk = SSM_W
    in_specs = [pl.BlockSpec((tb, wblk), lambda bi, i: (row(bi, i), PK_SZ // wblk)),
                pl.BlockSpec((tb, wblk), lambda bi, i: (row(bi, i), PK_SX // wblk)),
                pl.BlockSpec((tb, wblk), lambda bi, i: (row(bi, i), PK_SBC // wblk)),
                pl.BlockSpec((tb, LANES), lambda bi, i: (row(bi, i), PK_SM // LANES)),
                pl.BlockSpec((None, SSM_CONV, wblk), lambda bi, i: (layer, 0, 0)),
                pl.BlockSpec((None, SSM_CONV, wblk), lambda bi, i: (layer, 0, 1)),
                pl.BlockSpec((None, 1, wblk), lambda bi, i: (layer, 0, 0)),
                pl.BlockSpec((None, 1, wblk), lambda bi, i: (layer, 0, 1)),
                pl.BlockSpec((None, 1, LANES), lambda bi, i: (layer, 0, 0)),
                pl.BlockSpec((None, 1, LANES), lambda bi, i: (layer, 0, 0)),
                pl.BlockSpec((None, 1, SSM_W), lambda bi, i: (layer, 0, 0)),
                pl.BlockSpec((None, 1, SSM_W), lambda bi, i: (layer, 0, 0))]
    args = [proj, proj, proj, proj, conv_w, conv_w, conv_b.reshape(depth, 1, -1), conv_b.reshape(depth, 1, -1),
            _lane_row(dt_bias, SM_DT), _lane_row(a_log, SM_DT),
            jnp.repeat(d_skip.astype(F32), SSM_HEADDIM, axis=-1).reshape(depth, 1, SSM_W),
            norm_g.reshape(depth, 1, SSM_W)]
    if not zero_init:
        in_specs += [pl.BlockSpec((None, None, SSM_CONV - 1, wblk), lambda bi, i: (layer, bi, 0, 0)),
                     pl.BlockSpec((None, None, SSM_CONV - 1, wblk), lambda bi, i: (layer, bi, 0, 1)),
                     pl.BlockSpec((None, None, SSM_HEADS, SSM_STATE, SSM_HEADDIM),
                                  lambda bi, i: (layer, bi, 0, 0, 0))]
        args += [conv_buf, conv_buf, h0]
    y, h_fin = pl.pallas_call(
        functools.partial(_ssd_kernel, c=c, nc=nc, zero_init=zero_init),
        grid=(b, nt),
        in_specs=in_specs,
        out_specs=[pl.BlockSpec((tb, SSM_W), lambda bi, i: (row(bi, i), 0)),
                   pl.BlockSpec((None, SSM_HEADS, SSM_STATE, SSM_HEADDIM), lambda bi, i: (bi, 0, 0, 0))],
        out_shape=[jax.ShapeDtypeStruct((b * t, SSM_W), BF16),
                   jax.ShapeDtypeStruct((b, SSM_HEADS, SSM_STATE, SSM_HEADDIM), F32)],
        scratch_shapes=[pltpu.VMEM((tb + SUBLANES, wblk), F32)] * 2
        + [pltpu.VMEM((SSM_HEADS, SSM_STATE, SSM_HEADDIM), F32)],
        compiler_params=_cparams(("arbitrary", "arbitrary")),
        name="ssd",
    )(*args)
    return y, h_fin


def _rope_tables(pos0, t):
    half = SWA_HD // 2
    inv = ROPE_THETA ** (-jnp.arange(half, dtype=F32) / half)
    ang = (pos0 + jnp.arange(t, dtype=jnp.int32)).astype(F32)[:, None] * inv[None, :]
    cos, sin = jnp.cos(ang), jnp.sin(ang)
    return jnp.concatenate([cos, cos], axis=-1), jnp.concatenate([-sin, sin], axis=-1)


def _rope(x, cosf, sinf):
    return x * cosf + pltpu.roll(x, SWA_HD // 2, axis=1) * sinf


def _swa_prompt_kernel(q_ref, k_ref, v_ref, cos_ref, sin_ref, o_ref, kr_ref, qs, a_num, a_den, a_mx, *, t):
    qb = LANES
    cosf, sinf = cos_ref[...], sin_ref[...]
    qs[...] = _rope(q_ref[...], cosf, sinf) * (SWA_HD ** -0.5)
    kr_ref[...] = _rope(k_ref[...], cosf, sinf)

    m_i = lax.broadcasted_iota(jnp.int32, (qb, 2 * qb), 0)
    n_i = lax.broadcasted_iota(jnp.int32, (qb, 2 * qb), 1)
    mask_two = (n_i >= m_i) & (n_i <= m_i + qb)
    mask_one = (lax.broadcasted_iota(jnp.int32, (qb, qb), 1)
                <= lax.broadcasted_iota(jnp.int32, (qb, qb), 0))

    def block(q0, k0, nk, dil, merge):
        if dil == 1:
            qi, ki = pl.ds(q0, qb), pl.ds(k0, nk)
        else:
            qi, ki = pl.ds(q0, qb, stride=dil), pl.ds(k0, nk, stride=dil)
        sc = _mm_nt(qs[qi, :], kr_ref[ki, :])
        sc = jnp.where(mask_one if nk == qb else mask_two, sc, NEG_BIG)
        mx = jnp.max(sc, axis=-1, keepdims=True)
        p = jnp.exp(sc - mx)
        den = jnp.sum(p, axis=-1, keepdims=True)
        num = _mm(p, v_ref[ki, :])
        mx_b = jnp.broadcast_to(mx, (qb, LANES))
        den_b = jnp.broadcast_to(den, (qb, LANES))
        if merge:
            am = a_mx[qi, :]
            new_mx = jnp.maximum(am, mx_b)
            wa = jnp.exp(am - new_mx)
            wb = jnp.exp(mx_b - new_mx)
            a_num[qi, :] = a_num[qi, :] * wa + num * wb
            a_den[qi, :] = a_den[qi, :] * wa + den_b * wb
            a_mx[qi, :] = new_mx
        else:
            a_num[qi, :] = num
            a_den[qi, :] = den_b
            a_mx[qi, :] = mx_b

    for bi, (window, dil) in enumerate(DILATIONS):
        assert window == qb * dil and t % (qb * dil) == 0
        nblk = t // (qb * dil)
        merge = bi > 0

        def residue(r, carry, dil=dil, nblk=nblk, merge=merge):
            block(r, r, qb, dil, merge)
            if nblk > 1:
                def later(j, cc):
                    block(r + dil * qb * j, r + dil * qb * (j - 1), 2 * qb, dil, merge)
                    return cc
                lax.fori_loop(1, nblk, later, 0)
            return carry

        if dil == 1:
            residue(0, 0)
        else:
            lax.fori_loop(0, dil, residue, 0)

    o_ref[...] = (a_num[...] / a_den[...]).astype(o_ref.dtype)


def _swa_prompt(proj, b, t):
    cosf, sinf = _rope_tables(0, t)
    hq, hk, hv = PK_AQ // LANES, PK_AK // LANES, PK_AV // LANES
    return pl.pallas_call(
        functools.partial(_swa_prompt_kernel, t=t),
        grid=(b, SWA_HEADS),
        in_specs=[pl.BlockSpec((t, LANES), lambda bi, h: (bi, hq + h)),
                  pl.BlockSpec((t, LANES), lambda bi, h: (bi, hk + h)),
                  pl.BlockSpec((t, LANES), lambda bi, h: (bi, hv + h)),
                  pl.BlockSpec((t, LANES), lambda bi, h: (0, 0)),
                  pl.BlockSpec((t, LANES), lambda bi, h: (0, 0))],
        out_specs=[pl.BlockSpec((t, LANES), lambda bi, h: (bi, h)),
                   pl.BlockSpec((t, LANES), lambda bi, h: (bi, h))],
        out_shape=[jax.ShapeDtypeStruct((b * t, SWA_W), BF16),
                   jax.ShapeDtypeStruct((b * t, SWA_W), F32)],
        scratch_shapes=[pltpu.VMEM((t, LANES), F32)] * 4,
        compiler_params=_cparams(("arbitrary", "arbitrary")),
        name="swa_prompt",
    )(proj, proj, proj, cosf, sinf)


def _swa_sample_kernel(q_ref, k_ref, v_ref, cos_ref, sin_ref, kc3_ref, vc3_ref, kl_ref, vl_ref, o_ref, kr_ref,
                       *, t, past):
    (w1, d1), (w2, d2), (w3, d3) = DILATIONS
    n3 = kc3_ref.shape[0] * kc3_ref.shape[1]
    n_l = kl_ref.shape[0]
    assert d1 == 1 and t <= d3 and w3 == past and n_l == w2 and w1 <= n_l
    assert t & (t - 1) == 0 and d2 & (d2 - 1) == 0
    cosf, sinf = cos_ref[...], sin_ref[...]

    m3 = lax.broadcasted_iota(jnp.int32, (t, n3), 0)
    c3 = lax.broadcasted_iota(jnp.int32, (t, n3), 1)
    mask3 = (c3 & (t - 1)) == m3
    ml = lax.broadcasted_iota(jnp.int32, (t, n_l), 0)
    cl = lax.broadcasted_iota(jnp.int32, (t, n_l), 1)
    dist_l = n_l + ml - cl
    mask1l = dist_l <= w1
    mask2l = (dist_l <= w2) & ((dist_l & (d2 - 1)) == 0)
    mn = lax.broadcasted_iota(jnp.int32, (t, t), 0)
    cn = lax.broadcasted_iota(jnp.int32, (t, t), 1)
    dist_n = mn - cn
    mask1n = dist_n >= 0
    mask2n = (dist_n >= 0) & ((dist_n & (d2 - 1)) == 0)
    mask3n = dist_n == 0

    for h in range(SWA_HEADS):
        hs = slice(h * SWA_HD, (h + 1) * SWA_HD)
        qh = _rope(q_ref[:, hs], cosf, sinf) * (SWA_HD ** -0.5)
        kn = _rope(k_ref[:, hs], cosf, sinf)
        kr_ref[:, hs] = kn
        vn = v_ref[:, hs]
        k3 = kc3_ref[:, :, hs].reshape(n3, SWA_HD)
        v3 = vc3_ref[:, :, hs].reshape(n3, SWA_HD)
        s3 = _mm_nt(qh, k3)
        s_l = _mm_nt(qh, kl_ref[:, hs])
        s_n = _mm_nt(qh, kn)

        def branch(parts):
            mx = None
            for sc, mask in parts:
                cur = jnp.max(jnp.where(mask, sc, NEG_BIG), axis=-1, keepdims=True)
                mx = cur if mx is None else jnp.maximum(mx, cur)
            ps = [jnp.where(mask, jnp.exp(jnp.where(mask, sc, NEG_BIG) - mx), 0.0) for sc, mask in parts]
            den = sum(jnp.sum(p, axis=-1, keepdims=True) for p in ps)
            return mx, ps, den

        mx1, (p1l, p1n), den1 = branch([(s_l, mask1l), (s_n, mask1n)])
        mx2, (p2l, p2n), den2 = branch([(s_l, mask2l), (s_n, mask2n)])
        mx3, (p3c, p3n), den3 = branch([(s3, mask3), (s_n, mask3n)])
        mx = jnp.maximum(jnp.maximum(mx1, mx2), mx3)
        wt1, wt2, wt3 = jnp.exp(mx1 - mx), jnp.exp(mx2 - mx), jnp.exp(mx3 - mx)
        num = (_mm(p1l * wt1 + p2l * wt2, vl_ref[:, hs]) + _mm(p3c * wt3, v3)
               + _mm(p1n * wt1 + p2n * wt2 + p3n * wt3, vn))
        den = den1 * wt1 + den2 * wt2 + den3 * wt3
        o_ref[:, hs] = (num / den).astype(o_ref.dtype)


def _swa_sample(proj, cache_k, cache_v, layer, b, t, past):
    depth, _, l_cache, nh, hd = cache_k.shape
    assert l_cache == past and nh * hd == SWA_W
    (_, _), (w2, _), (_, d3) = DILATIONS
    cosf, sinf = _rope_tables(past, t)
    ck3 = cache_k.reshape(depth, b, l_cache // d3, d3, SWA_W)
    cv3 = cache_v.reshape(depth, b, l_cache // d3, d3, SWA_W)
    ckl = cache_k.reshape(depth, b, l_cache, SWA_W)
    cvl = cache_v.reshape(depth, b, l_cache, SWA_W)
    far = pl.BlockSpec((None, None, l_cache // d3, t, SWA_W), lambda bi: (layer, bi, 0, 0, 0))
    near = pl.BlockSpec((None, None, w2, SWA_W), lambda bi: (layer, bi, l_cache // w2 - 1, 0))
    return pl.pallas_call(
        functools.partial(_swa_sample_kernel, t=t, past=past),
        grid=(b,),
        in_specs=[pl.BlockSpec((t, SWA_W), lambda bi: (bi, PK_AQ // SWA_W)),
                  pl.BlockSpec((t, SWA_W), lambda bi: (bi, PK_AK // SWA_W)),
                  pl.BlockSpec((t, SWA_W), lambda bi: (bi, PK_AV // SWA_W)),
                  pl.BlockSpec((t, LANES), lambda bi: (0, 0)),
                  pl.BlockSpec((t, LANES), lambda bi: (0, 0)),
                  far, far, near, near],
        out_specs=[pl.BlockSpec((t, SWA_W), lambda bi: (bi, 0)),
                   pl.BlockSpec((t, SWA_W), lambda bi: (bi, 0))],
        out_shape=[jax.ShapeDtypeStruct((b * t, SWA_W), BF16),
                   jax.ShapeDtypeStruct((b * t, SWA_W), F32)],
        compiler_params=_cparams(("arbitrary",)),
        name="swa_sample",
    )(proj, proj, proj, cosf, sinf, ck3, cv3, ckl, cvl)


def _outproj_kernel(oa_ref, ob_ref, oc_ref, w_ref, x_ref, gate_ref, gpost_ref, gpre_ref, sh_ref, sc_ref,
                    xo_ref, h2_ref):
    mix = (jnp.dot(oa_ref[...], w_ref[0:GDN_W, :], preferred_element_type=F32)
           + jnp.dot(ob_ref[...], w_ref[GDN_W:GDN_W + SSM_W, :], preferred_element_type=F32)
           + jnp.dot(oc_ref[...], w_ref[GDN_W + SSM_W:, :], preferred_element_type=F32))
    y = _rms(mix, gpost_ref[...])
    x = x_ref[...] + gate_ref[...] * y.reshape(x_ref.shape)
    xo_ref[...] = x
    h2 = _rms(x, gpre_ref[...]) * (1.0 + sc_ref[...]) + sh_ref[...]
    h2_ref[...] = h2.reshape(h2_ref.shape).astype(h2_ref.dtype)


def _outproj(oa, ob, oc, w, x, mod, gpost, gpre, layer, bb, tt):
    b, t, d = x.shape
    nt = t // tt
    rows = bb * tt
    rmap = lambda bi, ti: (bi * nt + ti, 0)
    mspec = lambda k: pl.BlockSpec((None, bb, 1, d), lambda bi, ti: (layer, bi, 0, k))
    gspec = pl.BlockSpec((None, 1, d), lambda bi, ti: (layer, 0, 0))
    return pl.pallas_call(
        _outproj_kernel,
        grid=(b // bb, nt),
        in_specs=[pl.BlockSpec((rows, GDN_W), rmap), pl.BlockSpec((rows, SSM_W), rmap),
                  pl.BlockSpec((rows, SWA_W), rmap),
                  pl.BlockSpec((None, d, d), lambda bi, ti: (layer, 0, 0)),
                  pl.BlockSpec((bb, tt, d), lambda bi, ti: (bi, ti, 0)),
                  mspec(2), gspec, gspec, mspec(3), mspec(4)],
        out_specs=[pl.BlockSpec((bb, tt, d), lambda bi, ti: (bi, ti, 0)),
                   pl.BlockSpec((rows, d), rmap)],
        out_shape=[jax.ShapeDtypeStruct((b, t, d), F32), jax.ShapeDtypeStruct((b * t, d), BF16)],
        compiler_params=_cparams(("arbitrary", "arbitrary")),
        name="outproj",
    )(oa, ob, oc, w, x, mod, gpost, gpre, mod, mod)


def _ffn_up_kernel(*refs, tt, zero_init):
    if zero_init:
        (h_ref, wg_ref, wu_ref, cwg_ref, cwu_ref, cbg_ref, cbu_ref, act_ref, nsg_ref, nsu_ref, csg, csu) = refs
        stg_ref = stu_ref = None
    else:
        (h_ref, wg_ref, wu_ref, cwg_ref, cwu_ref, cbg_ref, cbu_ref, stg_ref, stu_ref,
         act_ref, nsg_ref, nsu_ref, csg, csu) = refs
    first = pl.program_id(2) == 0
    lo = SUBLANES - (FFN_CONV - 1)
    h = h_ref[...]

    def half(w_ref, cw_ref, cb_ref, st_ref, ns_ref, cs):
        bb = cs.shape[0]

        @pl.when(first)
        def _():
            if st_ref is None:
                cs[:, lo:SUBLANES, :] = jnp.zeros((bb, FFN_CONV - 1, cs.shape[2]), F32)
            else:
                cs[:, lo:SUBLANES, :] = st_ref[...]

        @pl.when(jnp.logical_not(first))
        def _():
            cs[:, lo:SUBLANES, :] = cs[:, tt + lo:tt + SUBLANES, :]

        up = jnp.dot(h, w_ref[...], preferred_element_type=F32)
        cs[:, SUBLANES:SUBLANES + tt, :] = up.reshape(bb, tt, up.shape[-1])
        y = cs[:, lo:lo + tt, :] * cw_ref[0:1, :]
        for j in range(1, FFN_CONV):
            y = y + cs[:, lo + j:lo + j + tt, :] * cw_ref[j:j + 1, :]
        ns_ref[...] = cs[:, tt + lo:tt + SUBLANES, :]
        return y + cb_ref[...]

    yg = half(wg_ref, cwg_ref, cbg_ref, stg_ref, nsg_ref, csg)
    yu = half(wu_ref, cwu_ref, cbu_ref, stu_ref, nsu_ref, csu)
    act = _silu(yg) * yu
    act_ref[...] = act.reshape(act_ref.shape).astype(act_ref.dtype)


def _ffn_up(h2, state, w_up, conv_w, conv_b, layer, b, t, bb, tt):
    zero_init = state is None
    d = h2.shape[1]
    tn = 512
    nh = D_FF // tn
    nt = t // tt
    rows = bb * tt
    depth = conv_w.shape[0]
    cb = conv_b.reshape(depth, 1, -1)
    in_specs = [pl.BlockSpec((rows, d), lambda bi, j, ti: (bi * nt + ti, 0)),
                pl.BlockSpec((None, d, tn), lambda bi, j, ti: (layer, 0, j)),
                pl.BlockSpec((None, d, tn), lambda bi, j, ti: (layer, 0, nh + j)),
                pl.BlockSpec((None, FFN_CONV, tn), lambda bi, j, ti: (layer, 0, j)),
                pl.BlockSpec((None, FFN_CONV, tn), lambda bi, j, ti: (layer, 0, nh + j)),
                pl.BlockSpec((None, 1, tn), lambda bi, j, ti: (layer, 0, j)),
                pl.BlockSpec((None, 1, tn), lambda bi, j, ti: (layer, 0, nh + j))]
    args = [h2, w_up, w_up, conv_w, conv_w, cb, cb]
    if not zero_init:
        in_specs += [pl.BlockSpec((None, bb, FFN_CONV - 1, tn), lambda bi, j, ti: (layer, bi, 0, j)),
                     pl.BlockSpec((None, bb, FFN_CONV - 1, tn), lambda bi, j, ti: (layer, bi, 0, nh + j))]
        args += [state, state]
    ns_spec = pl.BlockSpec((bb, FFN_CONV - 1, tn), lambda bi, j, ti: (bi, 0, j))
    act, nsg, nsu = pl.pallas_call(
        functools.partial(_ffn_up_kernel, tt=tt, zero_init=zero_init),
        grid=(b // bb, nh, nt),
        in_specs=in_specs,
        out_specs=[pl.BlockSpec((rows, tn), lambda bi, j, ti: (bi * nt + ti, j)), ns_spec, ns_spec],
        out_shape=[jax.ShapeDtypeStruct((b * t, D_FF), BF16),
                   jax.ShapeDtypeStruct((b, FFN_CONV - 1, D_FF), F32),
                   jax.ShapeDtypeStruct((b, FFN_CONV - 1, D_FF), F32)],
        scratch_shapes=[pltpu.VMEM((bb, tt + SUBLANES, tn), F32)] * 2,
        compiler_params=_cparams(("arbitrary", "arbitrary", "arbitrary")),
        name="ffn_up",
    )(*args)
    return act, jnp.concatenate([nsg, nsu], axis=-1)


def _ffn_down_kernel(a_ref, w_ref, x_ref, gate_ref, g_ref, o_ref, acc):
    k = pl.program_id(2)

    @pl.when(k == 0)
    def _():
        acc[...] = jnp.zeros_like(acc)

    acc[...] += jnp.dot(a_ref[...], w_ref[...], preferred_element_type=F32)

    @pl.when(k == pl.num_programs(2) - 1)
    def _():
        y = _rms(acc[...], g_ref[...])
        o_ref[...] = x_ref[...] + gate_ref[...] * y.reshape(x_ref.shape)


def _ffn_down(act, w_down, x, mod, g, layer, bb, tt):
    b, t, d = x.shape
    nt = t // tt
    rows = bb * tt
    tk = 1408
    return pl.pallas_call(
        _ffn_down_kernel,
        grid=(b // bb, nt, D_FF // tk),
        in_specs=[pl.BlockSpec((rows, tk), lambda bi, ti, k: (bi * nt + ti, k)),
                  pl.BlockSpec((None, tk, d), lambda bi, ti, k: (layer, k, 0)),
                  pl.BlockSpec((bb, tt, d), lambda bi, ti, k: (bi, ti, 0)),
                  pl.BlockSpec((None, bb, 1, d), lambda bi, ti, k: (layer, bi, 0, 5)),
                  pl.BlockSpec((None, 1, d), lambda bi, ti, k: (layer, 0, 0))],
        out_specs=pl.BlockSpec((bb, tt, d), lambda bi, ti, k: (bi, ti, 0)),
        out_shape=jax.ShapeDtypeStruct((b, t, d), F32),
        scratch_shapes=[pltpu.VMEM((rows, d), F32)],
        compiler_params=_cparams(("arbitrary", "arbitrary", "arbitrary")),
        name="ffn_down",
    )(act, w_down, x, mod, g)


def _pack_w_in(w_in):
    depth, d, _ = w_in.shape
    o_gb = 4 * GDN_W
    o_sz = o_gb + 2 * GDN_HEADS
    o_dt = o_sz + SSM_W + SSM_CONV_CH
    o_aq = o_dt + SSM_HEADS
    pad = jnp.zeros((depth, d, LANES - 2 * GDN_HEADS - SSM_HEADS), w_in.dtype)
    packed = jnp.concatenate([w_in[..., :o_gb], w_in[..., o_sz:o_dt], w_in[..., o_aq:],
                              w_in[..., o_gb:o_sz], w_in[..., o_dt:o_aq], pad], axis=-1)
    assert packed.shape[-1] == PK_COLS
    return packed.astype(BF16)


def _layer(x, mod, layer, states, wts, past, bb, tt):
    b, t, d = x.shape
    if states is None:
        gdn_conv = gdn_s = ssm_conv = ssm_h = cache_k = cache_v = ffn_conv = None
    else:
        gdn_conv, gdn_s, ssm_conv, ssm_h, cache_k, cache_v, ffn_conv = states
    proj = _inproj(x, mod, wts["g_pre_mix"], wts["w_in"], layer, bb, tt)
    o_a, new_gdn_s = _gdn(proj, gdn_conv, gdn_s, wts["gdn_conv_w"], wts["gdn_a_log"], wts["gdn_dt_bias"],
                          wts["gdn_norm_g"], layer, b, t)
    o_b, new_ssm_h = _ssd(proj, ssm_conv, ssm_h, wts["ssm_conv_w"], wts["ssm_conv_b"], wts["ssm_dt_bias"],
                          wts["ssm_a_log"], wts["ssm_d"], wts["ssm_norm_g"], layer, b, t)
    if states is None:
        o_c, k_rot = _swa_prompt(proj, b, t)
    else:
        o_c, k_rot = _swa_sample(proj, cache_k, cache_v, layer, b, t, past)
    x_mid, h2 = _outproj(o_a, o_b, o_c, wts["w_out"], x, mod, wts["g_post_mix"], wts["g_pre_ffn"], layer, bb, tt)
    act, new_ffn_conv = _ffn_up(h2, ffn_conv, wts["w_up"], wts["ffn_conv_w"], wts["ffn_conv_b"], layer, b, t, bb, tt)
    x_out = _ffn_down(act, wts["w_down"], x_mid, mod, wts["g_post_ffn"], layer, bb, tt)

    proj3 = proj.reshape(b, t, PK_COLS)
    keep = min(W_MAX, t)
    new_gdn_conv = proj3[:, t - (GDN_CONV - 1):, PK_GQ:PK_GQ + 3 * GDN_W]
    new_ssm_conv = proj3[:, t - (SSM_CONV - 1):, PK_SX:PK_SX + SSM_CONV_CH]
    new_k = k_rot.reshape(b, t, SWA_HEADS, SWA_HD)[:, t - keep:]
    new_v = proj3[:, t - keep:, PK_AV:PK_AV + SWA_W].reshape(b, keep, SWA_HEADS, SWA_HD)
    return x_out, (new_gdn_conv, new_gdn_s, new_ssm_conv, new_ssm_h, new_k, new_v, new_ffn_conv)


def kernel(x_prompt, x_sample, c_prompt, c_sample, state_gdn_conv, state_gdn, state_ssm_conv, state_ssm, cache_k, cache_v, state_ffn_conv, w_ada, b_ada, g_pre_mix, g_post_mix, g_pre_ffn, g_post_ffn, w_in, gdn_conv_w, gdn_a_log, gdn_dt_bias, gdn_norm_g, ssm_conv_w, ssm_conv_b, ssm_dt_bias, ssm_a_log, ssm_d, ssm_norm_g, w_out, w_up, ffn_conv_w, ffn_conv_b, w_down):
    depth = w_ada.shape[0]
    bp, tp, d = x_prompt.shape
    bs, ts, _ = x_sample.shape
    past = cache_k.shape[2]
    assert tp >= GDN_CONV and ts >= GDN_CONV and ts % SUBLANES == 0

    vec = lambda a: a.reshape(depth, 1, a.shape[-1])
    wts = dict(
        g_pre_mix=vec(g_pre_mix), g_post_mix=vec(g_post_mix), g_pre_ffn=vec(g_pre_ffn), g_post_ffn=vec(g_post_ffn),
        w_in=_pack_w_in(w_in), w_out=w_out.astype(BF16), w_up=w_up.astype(BF16), w_down=w_down.astype(BF16),
        gdn_conv_w=gdn_conv_w, gdn_a_log=gdn_a_log, gdn_dt_bias=gdn_dt_bias, gdn_norm_g=gdn_norm_g,
        ssm_conv_w=ssm_conv_w, ssm_conv_b=ssm_conv_b, ssm_dt_bias=ssm_dt_bias, ssm_a_log=ssm_a_log, ssm_d=ssm_d,
        ssm_norm_g=ssm_norm_g, ffn_conv_w=ffn_conv_w, ffn_conv_b=ffn_conv_b)

    mod = _ada(jnp.concatenate([c_prompt, c_sample], axis=0), w_ada, b_ada)
    mod_p = mod[:, :bp].reshape(depth, bp, 1, 6 * d)
    mod_s = mod[:, bp:].reshape(depth, bs, 1, 6 * d)
    s_states = (state_gdn_conv, state_gdn, state_ssm_conv, state_ssm, cache_k, cache_v, state_ffn_conv)

    tt_p = math.gcd(tp, 512)
    bb_s = math.gcd(bs, 512 // ts)
    y_p, y_s = x_prompt, x_sample
    p_rows, s_rows = [], []
    for layer in range(depth):
        y_p, outs_p = _layer(y_p, mod_p, layer, None, wts, 0, 1, tt_p)
        y_s, outs_s = _layer(y_s, mod_s, layer, s_states, wts, past, bb_s, ts)
        p_rows.append(outs_p)
        s_rows.append(outs_s)
    p_out = [jnp.stack(a) for a in zip(*p_rows)]
    s_out = [jnp.stack(a) for a in zip(*s_rows)]
    return (y_p, y_s, *p_out, *s_out)
```

```python
import functools
import math

import jax
import jax.numpy as jnp
import numpy as np
from jax import lax
from jax.experimental import pallas as pl
from jax.experimental.pallas import tpu as pltpu

F32 = jnp.float32
BF16 = jnp.bfloat16

D_MODEL = 2048
MIX_UNIT = D_MODEL // 8
GDN_W = 3 * MIX_UNIT
SSM_W = 2 * MIX_UNIT
SWA_W = 3 * MIX_UNIT
GDN_DK = 128
GDN_DV = 128
GDN_HEADS = GDN_W // GDN_DV
GDN_CONV = 4
GDN_CHUNK = 64
SSM_HEADDIM = 64
SSM_HEADS = SSM_W // SSM_HEADDIM
SSM_GROUPS = 2
SSM_STATE = 128
SSM_CONV = 4
SSM_CHUNK = 64
SSM_CONV_CH = SSM_W + 2 * SSM_GROUPS * SSM_STATE
SWA_HD = 128
SWA_HEADS = SWA_W // SWA_HD
DILATIONS = ((128, 1), (512, 4), (2048, 16))
W_MAX = 2048
ROPE_THETA = 10000.0
D_FF = 11 * D_MODEL // 4
FFN_CONV = 3
NORM_EPS = 1e-6

LANES = 128
SUBLANES = 8
VMEM_LIMIT = 56 * 1024 * 1024

PK_GQ = 0
PK_GK = PK_GQ + GDN_W
PK_GV = PK_GK + GDN_W
PK_GZ = PK_GV + GDN_W
PK_SZ = PK_GZ + GDN_W
PK_SX = PK_SZ + SSM_W
PK_SBC = PK_SX + SSM_W
PK_AQ = PK_SX + SSM_CONV_CH
PK_AK = PK_AQ + SWA_W
PK_AV = PK_AK + SWA_W
PK_SM = PK_AV + SWA_W
PK_COLS = PK_SM + LANES
SM_BETA = 0
SM_A = GDN_HEADS
SM_DT = 2 * GDN_HEADS
NEG_BIG = -1e30


def _cparams(sem):
    return pltpu.CompilerParams(dimension_semantics=sem, vmem_limit_bytes=VMEM_LIMIT)


def _sigmoid(x):
    return 1.0 / (1.0 + jnp.exp(-x))


def _silu(x):
    return x * _sigmoid(x)


def _softplus(x):
    return jnp.maximum(x, 0.0) + jnp.log1p(jnp.exp(-jnp.abs(x)))


def _mm(a, b):
    return jnp.dot(a.astype(BF16), b.astype(BF16), preferred_element_type=F32)


def _mm_nt(a, b):
    return lax.dot_general(a.astype(BF16), b.astype(BF16), (((1,), (1,)), ((), ())), preferred_element_type=F32)


def _mm_hi(a, b):
    return jnp.dot(a, b, preferred_element_type=F32, precision=lax.Precision.HIGHEST)


def _rms(x, g):
    return x * lax.rsqrt(jnp.mean(x * x, axis=-1, keepdims=True) + NORM_EPS) * g


def _ada_kernel(c_ref, w_ref, b_ref, o_ref):
    a = _silu(c_ref[...]).astype(BF16)
    o_ref[...] = jnp.dot(a, w_ref[...].astype(BF16), preferred_element_type=F32) + b_ref[...]


def _ada(c_all, w_ada, b_ada):
    depth, d, n = w_ada.shape
    r = c_all.shape[0]
    tn = 1024
    return pl.pallas_call(
        _ada_kernel,
        grid=(depth, n // tn),
        in_specs=[pl.BlockSpec((r, d), lambda l, j: (0, 0)),
                  pl.BlockSpec((None, d, tn), lambda l, j: (l, 0, j)),
                  pl.BlockSpec((None, 1, tn), lambda l, j: (l, 0, j))],
        out_specs=pl.BlockSpec((None, r, tn), lambda l, j: (l, 0, j)),
        out_shape=jax.ShapeDtypeStruct((depth, r, n), F32),
        compiler_params=_cparams(("arbitrary", "arbitrary")),
        name="ada",
    )(c_all, w_ada, b_ada.reshape(depth, 1, n))


def _inproj_kernel(x_ref, sh_ref, sc_ref, g_ref, w_ref, o_ref, h_scr):
    @pl.when(pl.program_id(2) == 0)
    def _():
        h = _rms(x_ref[...], g_ref[...]) * (1.0 + sc_ref[...]) + sh_ref[...]
        h_scr[...] = h.reshape(h_scr.shape).astype(BF16)

    o_ref[...] = jnp.dot(h_scr[...], w_ref[...], preferred_element_type=F32)


def _inproj(x, mod, g, w, layer, bb, tt):
    b, t, d = x.shape
    n = w.shape[-1]
    tn = 1408
    nt = t // tt
    return pl.pallas_call(
        _inproj_kernel,
        grid=(b // bb, nt, n // tn),
        in_specs=[pl.BlockSpec((bb, tt, d), lambda bi, ti, j: (bi, ti, 0)),
                  pl.BlockSpec((None, bb, 1, d), lambda bi, ti, j: (layer, bi, 0, 0)),
                  pl.BlockSpec((None, bb, 1, d), lambda bi, ti, j: (layer, bi, 0, 1)),
                  pl.BlockSpec((None, 1, d), lambda bi, ti, j: (layer, 0, 0)),
                  pl.BlockSpec((None, d, tn), lambda bi, ti, j: (layer, 0, j))],
        out_specs=pl.BlockSpec((bb * tt, tn), lambda bi, ti, j: (bi * nt + ti, j)),
        out_shape=jax.ShapeDtypeStruct((b * t, n), F32),
        scratch_shapes=[pltpu.VMEM((bb * tt, d), BF16)],
        compiler_params=_cparams(("arbitrary", "arbitrary", "arbitrary")),
        name="inproj",
    )(x, mod, mod, g, w)


def _conv_from_scratch(xs, w_ref, width, t):
    lo = SUBLANES - (width - 1)
    y = xs[lo:lo + t, :] * w_ref[0:1, :]
    for j in range(1, width):
        y = y + xs[lo + j:lo + j + t, :] * w_ref[j:j + 1, :]
    return y


def _conv_stage(xs, x_new, hist_ref, first, width, t):
    lo = SUBLANES - (width - 1)

    @pl.when(first)
    def _():
        if hist_ref is None:
            xs[lo:SUBLANES, :] = jnp.zeros((width - 1, xs.shape[1]), F32)
        else:
            xs[lo:SUBLANES, :] = hist_ref[...]

    @pl.when(jnp.logical_not(first))
    def _():
        xs[lo:SUBLANES, :] = xs[t + lo:t + SUBLANES, :]

    xs[SUBLANES:SUBLANES + t, :] = x_new


def _tri_masks(c):
    row = lax.broadcasted_iota(jnp.int32, (c, c), 0)
    col = lax.broadcasted_iota(jnp.int32, (c, c), 1)
    return row == col, row >= col, row > col, row <= col


def _bdot(a, b, ca, cb):
    return lax.dot_general(a, b, (((ca,), (cb,)), ((0,), (0,))), preferred_element_type=F32)


def _bmm(a, b):
    return _bdot(a.astype(BF16), b.astype(BF16), 2, 1)


def _bmm_nt(a, b):
    return _bdot(a.astype(BF16), b.astype(BF16), 2, 2)


def _bmm_tn(a, b):
    return _bdot(a.astype(BF16), b.astype(BF16), 1, 1)


def _split_bf16(a):
    hi = a.astype(BF16)
    return hi, (a - hi.astype(F32)).astype(BF16)


def _bmm3(a, b):
    ah, al = _split_bf16(a)
    bh, bl = _split_bf16(b)
    return _bdot(ah, bh, 2, 1) + _bdot(ah, bl, 2, 1) + _bdot(al, bh, 2, 1)


def _unit_lower_inverse(a, eye_f, c):
    x = eye_f - a
    p = _bmm3(a, a)
    n = 2
    while True:
        x = x + _bmm3(x, p)
        n *= 2
        if n >= c:
            return x
        p = _bmm3(p, p)


def _gdn_prep(q, k, v, beta, g, c):
    eye, causal, strict, upper = _tri_masks(c)
    g_row = jnp.sum(jnp.where(eye, g, 0.0), axis=1, keepdims=True)
    cg_col = jnp.sum(jnp.where(causal, g_row, 0.0), axis=2, keepdims=True)
    cg_row = jnp.sum(jnp.where(upper, g, 0.0), axis=1, keepdims=True)
    gam = jnp.where(causal, jnp.exp(jnp.where(causal, cg_col - cg_row, 0.0)), 0.0)
    kb = k * beta
    amat = jnp.where(strict, _bmm_nt(kb, k) * gam, 0.0)
    tinv = _unit_lower_inverse(amat, eye.astype(F32), c)
    ecg = jnp.exp(cg_col)
    u = _bmm3(tinv, v * beta)
    w = _bmm3(tinv, kb * ecg)
    qk = _bmm_nt(q, k) * gam
    cg_last = cg_col[:, c - 1:c, :]
    kd = k * jnp.exp(cg_last - cg_col)
    return u, w, qk, q * ecg, kd, jnp.exp(cg_last)


def _gdn_gates(b_raw, a_raw, a_log, dt_bias):
    a_neg = -jnp.exp(jnp.full((1, 1), a_log, F32))
    return _sigmoid(b_raw), a_neg * _softplus(a_raw + dt_bias)


def _l2norm(x):
    return x * lax.rsqrt(jnp.sum(x * x, axis=-1, keepdims=True) + NORM_EPS)


def _gdn_seq_kernel(alog_ref, dtb_ref, q_ref, k_ref, v_ref, z_ref, sm_ref, wq_ref, wk_ref, wv_ref, ng_ref,
                    o_ref, sfin_ref, xq, xk, xv, s_scr, *, c, nc):
    h = pl.program_id(1)
    tb = c * nc
    first = pl.program_id(2) == 0

    @pl.when(first)
    def _():
        s_scr[...] = jnp.zeros_like(s_scr)

    _conv_stage(xq, q_ref[...], None, first, GDN_CONV, tb)
    _conv_stage(xk, k_ref[...], None, first, GDN_CONV, tb)
    _conv_stage(xv, v_ref[...], None, first, GDN_CONV, tb)
    q = _l2norm(_silu(_conv_from_scratch(xq, wq_ref, GDN_CONV, tb))) * (GDN_DK ** -0.5)
    k = _l2norm(_silu(_conv_from_scratch(xk, wk_ref, GDN_CONV, tb)))
    v = _silu(_conv_from_scratch(xv, wv_ref, GDN_CONV, tb))

    sm = sm_ref[...]
    lane = lax.broadcasted_iota(jnp.int32, sm.shape, 1)
    b_raw = jnp.sum(jnp.where(lane == SM_BETA + h, sm, 0.0), axis=1, keepdims=True)
    a_raw = jnp.sum(jnp.where(lane == SM_A + h, sm, 0.0), axis=1, keepdims=True)
    beta, g = _gdn_gates(b_raw, a_raw, alog_ref[h], dtb_ref[h])

    chunks = lambda a: a.reshape(nc, c, a.shape[-1])
    u, w, qk, qg, kd, gl = _gdn_prep(chunks(q), chunks(k), chunks(v), chunks(beta), chunks(g), c)

    ng = ng_ref[...]
    s = s_scr[...]
    for ci in range(nc):
        sl = slice(ci * c, (ci + 1) * c)
        ws = _mm(jnp.concatenate([w[ci], qg[ci]], axis=0), s)
        vnew = u[ci] - ws[:c]
        r = _mm(jnp.concatenate([qk[ci], kd[ci].T], axis=0), vnew)
        s = s * gl[ci] + r[c:]
        o = _rms(ws[c:] + r[:c], ng) * _silu(z_ref[sl, :])
        o_ref[sl, :] = o.astype(o_ref.dtype)
    s_scr[...] = s
    sfin_ref[...] = s


def _gdn_seq(proj, conv_w, a_log, dt_bias, norm_g, layer, b, t):
    c = math.gcd(t, GDN_CHUNK)
    nc = math.gcd(t // c, 8)
    tb = c * nc
    nt = t // tb
    hq, hk, hv, hz = PK_GQ // LANES, PK_GK // LANES, PK_GV // LANES, PK_GZ // LANES
    smem = pl.BlockSpec(memory_space=pltpu.SMEM)
    cblk = lambda off: pl.BlockSpec((tb, LANES), lambda bi, h, i: (bi * nt + i, off + h))
    wblk = lambda off: pl.BlockSpec((None, GDN_CONV, LANES), lambda bi, h, i: (layer, 0, off + h))
    return pl.pallas_call(
        functools.partial(_gdn_seq_kernel, c=c, nc=nc),
        grid=(b, GDN_HEADS, nt),
        in_specs=[smem, smem, cblk(hq), cblk(hk), cblk(hv), cblk(hz),
                  pl.BlockSpec((tb, LANES), lambda bi, h, i: (bi * nt + i, PK_SM // LANES)),
                  wblk(0), wblk(GDN_HEADS), wblk(2 * GDN_HEADS),
                  pl.BlockSpec((None, 1, LANES), lambda bi, h, i: (layer, 0, 0))],
        out_specs=[cblk(0), pl.BlockSpec((None, None, GDN_DK, GDN_DV), lambda bi, h, i: (bi, h, 0, 0))],
        out_shape=[jax.ShapeDtypeStruct((b * t, GDN_W), BF16),
                   jax.ShapeDtypeStruct((b, GDN_HEADS, GDN_DK, GDN_DV), F32)],
        scratch_shapes=[pltpu.VMEM((tb + SUBLANES, LANES), F32)] * 3 + [pltpu.VMEM((GDN_DK, GDN_DV), F32)],
        compiler_params=_cparams(("arbitrary", "arbitrary", "arbitrary")),
        name="gdn_seq",
    )(a_log[layer], dt_bias[layer], proj, proj, proj, proj, proj, conv_w, conv_w, conv_w,
      norm_g.reshape(norm_g.shape[0], 1, LANES))


def _gdn_step_kernel(alog_ref, dtb_ref, x_ref, z_ref, sm_ref, w_ref, ng_ref, hist_ref, s0_ref,
                     o_ref, sfin_ref, cs, *, bb, t):
    lo = SUBLANES - (GDN_CONV - 1)
    cs[:, lo:SUBLANES, :] = hist_ref[...]
    cs[:, SUBLANES:SUBLANES + t, :] = x_ref[...].reshape(bb, t, 3 * GDN_W)
    y = cs[:, lo:lo + t, :] * w_ref[0:1, :]
    for j in range(1, GDN_CONV):
        y = y + cs[:, lo + j:lo + j + t, :] * w_ref[j:j + 1, :]
    y = _silu(y)
    sm = sm_ref[...].reshape(bb, t, LANES)
    ng = ng_ref[...]
    for h in range(GDN_HEADS):
        hs = slice(h * GDN_DK, (h + 1) * GDN_DK)
        q = _l2norm(y[:, :, hs]) * (GDN_DK ** -0.5)
        k = _l2norm(y[:, :, GDN_W + h * GDN_DK:GDN_W + (h + 1) * GDN_DK])
        v = y[:, :, 2 * GDN_W + h * GDN_DV:2 * GDN_W + (h + 1) * GDN_DV]
        beta, g = _gdn_gates(sm[:, :, SM_BETA + h:SM_BETA + h + 1], sm[:, :, SM_A + h:SM_A + h + 1],
                             alog_ref[h], dtb_ref[h])
        u, w, qk, qg, kd, gl = _gdn_prep(q, k, v, beta, g, t)
        s = s0_ref[:, h]
        vnew = u - _bmm(w, s)
        o = _bmm(qg, s) + _bmm(qk, vnew)
        sfin_ref[:, h] = s * gl + _bmm_tn(kd, vnew)
        o = _rms(o, ng) * _silu(z_ref[:, hs].reshape(bb, t, GDN_DV))
        o_ref[:, hs] = o.reshape(bb * t, GDN_DV).astype(o_ref.dtype)


def _gdn_step(proj, conv_buf, s0, conv_w, a_log, dt_bias, norm_g, layer, b, t):
    assert t % SUBLANES == 0 and GDN_CHUNK % t == 0 and t >= GDN_CONV - 1
    bb = math.gcd(b, 8)
    rows = bb * t
    smem = pl.BlockSpec(memory_space=pltpu.SMEM)
    return pl.pallas_call(
        functools.partial(_gdn_step_kernel, bb=bb, t=t),
        grid=(b // bb,),
        in_specs=[smem, smem,
                  pl.BlockSpec((rows, 3 * GDN_W), lambda bi: (bi, PK_GQ // (3 * GDN_W))),
                  pl.BlockSpec((rows, GDN_W), lambda bi: (bi, PK_GZ // GDN_W)),
                  pl.BlockSpec((rows, LANES), lambda bi: (bi, PK_SM // LANES)),
                  pl.BlockSpec((None, GDN_CONV, 3 * GDN_W), lambda bi: (layer, 0, 0)),
                  pl.BlockSpec((None, 1, LANES), lambda bi: (layer, 0, 0)),
                  pl.BlockSpec((None, bb, GDN_CONV - 1, 3 * GDN_W), lambda bi: (layer, bi, 0, 0)),
                  pl.BlockSpec((None, bb, GDN_HEADS, GDN_DK, GDN_DV), lambda bi: (layer, bi, 0, 0, 0))],
        out_specs=[pl.BlockSpec((rows, GDN_W), lambda bi: (bi, 0)),
                   pl.BlockSpec((bb, GDN_HEADS, GDN_DK, GDN_DV), lambda bi: (bi, 0, 0, 0))],
        out_shape=[jax.ShapeDtypeStruct((b * t, GDN_W), BF16),
                   jax.ShapeDtypeStruct((b, GDN_HEADS, GDN_DK, GDN_DV), F32)],
        scratch_shapes=[pltpu.VMEM((bb, t + SUBLANES, 3 * GDN_W), F32)],
        compiler_params=_cparams(("arbitrary",)),
        name="gdn_step",
    )(a_log[layer], dt_bias[layer], proj, proj, proj, conv_w, norm_g.reshape(norm_g.shape[0], 1, LANES),
      conv_buf, s0)


def _ssd_kernel(*refs, c, nc, zero_init):
    if zero_init:
        (z_ref, x_ref, bc_ref, sm_ref, wx_ref, wbc_ref, bx_ref, bbc_ref, dtb_ref, alog_ref, dsk_ref, ng_ref,
         y_ref, hfin_ref, xs, xbc, h_scr) = refs
        cx_ref = cbc_ref = h0_ref = None
    else:
        (z_ref, x_ref, bc_ref, sm_ref, wx_ref, wbc_ref, bx_ref, bbc_ref, dtb_ref, alog_ref, dsk_ref, ng_ref,
         cx_ref, cbc_ref, h0_ref, y_ref, hfin_ref, xs, xbc, h_scr) = refs
    i = pl.program_id(1)
    tb = c * nc
    first = i == 0

    @pl.when(first)
    def _():
        if zero_init:
            h_scr[...] = jnp.zeros_like(h_scr)
        else:
            h_scr[...] = h0_ref[...]

    _conv_stage(xs, x_ref[...], cx_ref, first, SSM_CONV, tb)
    _conv_stage(xbc, bc_ref[...], cbc_ref, first, SSM_CONV, tb)
    xv = _silu(_conv_from_scratch(xs, wx_ref, SSM_CONV, tb) + bx_ref[...])
    bcv = _silu(_conv_from_scratch(xbc, wbc_ref, SSM_CONV, tb) + bbc_ref[...])
    gs = SSM_GROUPS * SSM_STATE

    dt_all = _softplus(sm_ref[...] + dtb_ref[...])
    da_all = dt_all * (-jnp.exp(alog_ref[...]))
    _, causal, _, _ = _tri_masks(c)
    tril_f = causal.astype(F32)
    rep = SSM_HEADS // SSM_GROUPS
    gw = SSM_W // SSM_GROUPS
    for ci in range(nc):
        sl = slice(ci * c, (ci + 1) * c)
        cg_all = _mm_hi(tril_f, da_all[sl])
        cg_t = cg_all.T
        ys = []
        for grp in range(SSM_GROUPS):
            bm = bcv[sl, grp * SSM_STATE:(grp + 1) * SSM_STATE]
            cm = bcv[sl, gs + grp * SSM_STATE:gs + (grp + 1) * SSM_STATE]
            cb = _mm_nt(cm, bm)
            for hh in range(rep):
                hd = grp * rep + hh
                ln = SM_DT + hd
                cg_col = cg_all[:, ln:ln + 1]
                cg_row = cg_t[ln:ln + 1, :]
                lmat = jnp.where(causal, jnp.exp(jnp.where(causal, cg_col - cg_row, 0.0)), 0.0)
                xh = xv[sl, hd * SSM_HEADDIM:(hd + 1) * SSM_HEADDIM]
                xdt = xh * dt_all[sl, ln:ln + 1]
                cg_last = cg_col[c - 1:c, :]
                hst = h_scr[hd]
                y = _mm(cb * lmat, xdt) + _mm(cm * jnp.exp(cg_col), hst)
                h_scr[hd] = hst * jnp.exp(cg_last) + _mm((bm * jnp.exp(cg_last - cg_col)).T, xdt)
                ys.append(y)
        yc = jnp.concatenate(ys, axis=-1) + dsk_ref[...] * xv[sl]
        yc = yc * _silu(z_ref[sl, :])
        ng = ng_ref[...]
        outs = [_rms(yc[:, gi * gw:(gi + 1) * gw], ng[:, gi * gw:(gi + 1) * gw]) for gi in range(SSM_GROUPS)]
        y_ref[sl, :] = jnp.concatenate(outs, axis=-1).astype(y_ref.dtype)

    hfin_ref[...] = h_scr[...]


def _lane_row(vals, offset):
    depth, n = vals.shape
    return jnp.pad(vals.astype(F32), ((0, 0), (offset, LANES - offset - n))).reshape(depth, 1, LANES)


def _ssd(proj, conv_buf, h0, conv_w, conv_b, dt_bias, a_log, d_skip, norm_g, layer, b, t):
    zero_init = h0 is None
    c = math.gcd(t, SSM_CHUNK)
    nc = max(1, min(4, t // c))
    tb = c * nc
    nt = t // tb
    depth = conv_w.shape[0]
    row = lambda bi, i: bi * nt + i
    wblk = SSM_W
    in_specs = [pl.BlockSpec((tb, wblk), lambda bi, i: (row(bi, i), PK_SZ // wblk)),
                pl.BlockSpec((tb, wblk), lambda bi, i: (row(bi, i), PK_SX // wblk)),
                pl.BlockSpec((tb, wblk), lambda bi, i: (row(bi, i), PK_SBC // wblk)),
                pl.BlockSpec((tb, LANES), lambda bi, i: (row(bi, i), PK_SM // LANES)),
                pl.BlockSpec((None, SSM_CONV, wblk), lambda bi, i: (layer, 0, 0)),
                pl.BlockSpec((None, SSM_CONV, wblk), lambda bi, i: (layer, 0, 1)),
                pl.BlockSpec((None, 1, wblk), lambda bi, i: (layer, 0, 0)),
                pl.BlockSpec((None, 1, wblk), lambda bi, i: (layer, 0, 1)),
                pl.BlockSpec((None, 1, LANES), lambda bi, i: (layer, 0, 0)),
                pl.BlockSpec((None, 1, LANES), lambda bi, i: (layer, 0, 0)),
                pl.BlockSpec((None, 1, SSM_W), lambda bi, i: (layer, 0, 0)),
                pl.BlockSpec((None, 1, SSM_W), lambda bi, i: (layer, 0, 0))]
    args = [proj, proj, proj, proj, conv_w, conv_w, conv_b.reshape(depth, 1, -1), conv_b.reshape(depth, 1, -1),
            _lane_row(dt_bias, SM_DT), _lane_row(a_log, SM_DT),
            jnp.repeat(d_skip.astype(F32), SSM_HEADDIM, axis=-1).reshape(depth, 1, SSM_W),
            norm_g.reshape(depth, 1, SSM_W)]
    if not zero_init:
        in_specs += [pl.BlockSpec((None, None, SSM_CONV - 1, wblk), lambda bi, i: (layer, bi, 0, 0)),
                     pl.BlockSpec((None, None, SSM_CONV - 1, wblk), lambda bi, i: (layer, bi, 0, 1)),
                     pl.BlockSpec((None, None, SSM_HEADS, SSM_STATE, SSM_HEADDIM),
                                  lambda bi, i: (layer, bi, 0, 0, 0))]
        args += [conv_buf, conv_buf, h0]
    y, h_fin = pl.pallas_call(
        functools.partial(_ssd_kernel, c=c, nc=nc, zero_init=zero_init),
        grid=(b, nt),
        in_specs=in_specs,
        out_specs=[pl.BlockSpec((tb, SSM_W), lambda bi, i: (row(bi, i), 0)),
                   pl.BlockSpec((None, SSM_HEADS, SSM_STATE, SSM_HEADDIM), lambda bi, i: (bi, 0, 0, 0))],
        out_shape=[jax.ShapeDtypeStruct((b * t, SSM_W), BF16),
                   jax.ShapeDtypeStruct((b, SSM_HEADS, SSM_STATE, SSM_HEADDIM), F32)],
        scratch_shapes=[pltpu.VMEM((tb + SUBLANES, wblk), F32)] * 2
        + [pltpu.VMEM((SSM_HEADS, SSM_STATE, SSM_HEADDIM), F32)],
        compiler_params=_cparams(("arbitrary", "arbitrary")),
        name="ssd",
    )(*args)
    return y, h_fin


def _rope_tables(pos0, t):
    half = SWA_HD // 2
    inv = ROPE_THETA ** (-jnp.arange(half, dtype=F32) / half)
    ang = (pos0 + jnp.arange(t, dtype=jnp.int32)).astype(F32)[:, None] * inv[None, :]
    cos, sin = jnp.cos(ang), jnp.sin(ang)
    return jnp.concatenate([cos, cos], axis=-1), jnp.concatenate([-sin, sin], axis=-1)


def _rope(x, cosf, sinf):
    return x * cosf + pltpu.roll(x, SWA_HD // 2, axis=1) * sinf


def _swa_prompt_kernel(q_ref, k_ref, v_ref, cos_ref, sin_ref, o_ref, kr_ref, qs, a_num, a_den, a_mx, *, t):
    qb = LANES
    cosf, sinf = cos_ref[...], sin_ref[...]
    qs[...] = _rope(q_ref[...], cosf, sinf) * (SWA_HD ** -0.5)
    kr_ref[...] = _rope(k_ref[...], cosf, sinf)

    m_i = lax.broadcasted_iota(jnp.int32, (qb, 2 * qb), 0)
    n_i = lax.broadcasted_iota(jnp.int32, (qb, 2 * qb), 1)
    mask_two = (n_i >= m_i) & (n_i <= m_i + qb)
    mask_one = (lax.broadcasted_iota(jnp.int32, (qb, qb), 1)
                <= lax.broadcasted_iota(jnp.int32, (qb, qb), 0))

    def block(q0, k0, nk, dil, merge):
        if dil == 1:
            qi, ki = pl.ds(q0, qb), pl.ds(k0, nk)
        else:
            qi, ki = pl.ds(q0, qb, stride=dil), pl.ds(k0, nk, stride=dil)
        sc = _mm_nt(qs[qi, :], kr_ref[ki, :])
        sc = jnp.where(mask_one if nk == qb else mask_two, sc, NEG_BIG)
        mx = jnp.max(sc, axis=-1, keepdims=True)
        p = jnp.exp(sc - mx)
        den = jnp.sum(p, axis=-1, keepdims=True)
        num = _mm(p, v_ref[ki, :])
        mx_b = jnp.broadcast_to(mx, (qb, LANES))
        den_b = jnp.broadcast_to(den, (qb, LANES))
        if merge:
            am = a_mx[qi, :]
            new_mx = jnp.maximum(am, mx_b)
            wa = jnp.exp(am - new_mx)
            wb = jnp.exp(mx_b - new_mx)
            a_num[qi, :] = a_num[qi, :] * wa + num * wb
            a_den[qi, :] = a_den[qi, :] * wa + den_b * wb
            a_mx[qi, :] = new_mx
        else:
            a_num[qi, :] = num
            a_den[qi, :] = den_b
            a_mx[qi, :] = mx_b

    for bi, (window, dil) in enumerate(DILATIONS):
        assert window == qb * dil and t % (qb * dil) == 0
        nblk = t // (qb * dil)
        merge = bi > 0

        def residue(r, carry, dil=dil, nblk=nblk, merge=merge):
            block(r, r, qb, dil, merge)
            if nblk > 1:
                def later(j, cc):
                    block(r + dil * qb * j, r + dil * qb * (j - 1), 2 * qb, dil, merge)
                    return cc
                lax.fori_loop(1, nblk, later, 0)
            return carry

        if dil == 1:
            residue(0, 0)
        else:
            lax.fori_loop(0, dil, residue, 0)

    o_ref[...] = (a_num[...] / a_den[...]).astype(o_ref.dtype)


def _swa_prompt(proj, b, t):
    cosf, sinf = _rope_tables(0, t)
    hq, hk, hv = PK_AQ // LANES, PK_AK // LANES, PK_AV // LANES
    return pl.pallas_call(
        functools.partial(_swa_prompt_kernel, t=t),
        grid=(b, SWA_HEADS),
        in_specs=[pl.BlockSpec((t, LANES), lambda bi, h: (bi, hq + h)),
                  pl.BlockSpec((t, LANES), lambda bi, h: (bi, hk + h)),
                  pl.BlockSpec((t, LANES), lambda bi, h: (bi, hv + h)),
                  pl.BlockSpec((t, LANES), lambda bi, h: (0, 0)),
                  pl.BlockSpec((t, LANES), lambda bi, h: (0, 0))],
        out_specs=[pl.BlockSpec((t, LANES), lambda bi, h: (bi, h)),
                   pl.BlockSpec((t, LANES), lambda bi, h: (bi, h))],
        out_shape=[jax.ShapeDtypeStruct((b * t, SWA_W), BF16),
                   jax.ShapeDtypeStruct((b * t, SWA_W), F32)],
        scratch_shapes=[pltpu.VMEM((t, LANES), F32)] * 4,
        compiler_params=_cparams(("arbitrary", "arbitrary")),
        name="swa_prompt",
    )(proj, proj, proj, cosf, sinf)


def _swa_sample_kernel(q_ref, k_ref, v_ref, cos_ref, sin_ref, ck_hbm, cv_hbm, o_ref, kr_ref,
                       kf, vf, kn, vn, sem, *, layer, t, past):
    (w1, d1), (w2, d2), (w3, d3) = DILATIONS
    n3 = kf.shape[2] * kf.shape[3]
    n_l = kn.shape[2] * kn.shape[3]
    assert d1 == 1 and t <= d3 and w3 == past and n_l == w2 and w1 <= n_l
    assert t & (t - 1) == 0 and d2 & (d2 - 1) == 0
    cosf, sinf = cos_ref[...], sin_ref[...]
    bi = pl.program_id(0)
    slot = bi % 2

    def copies(b_idx, s):
        near_kk = pl.ds((past - n_l) // d3, n_l // d3)
        out = []
        for h in range(SWA_HEADS):
            out += [pltpu.make_async_copy(ck_hbm.at[layer, b_idx, :, pl.ds(0, t), h, :], kf.at[s, h], sem.at[s, 0, h]),
                    pltpu.make_async_copy(cv_hbm.at[layer, b_idx, :, pl.ds(0, t), h, :], vf.at[s, h], sem.at[s, 1, h]),
                    pltpu.make_async_copy(ck_hbm.at[layer, b_idx, near_kk, :, h, :], kn.at[s, h], sem.at[s, 2, h]),
                    pltpu.make_async_copy(cv_hbm.at[layer, b_idx, near_kk, :, h, :], vn.at[s, h], sem.at[s, 3, h])]
        return out

    @pl.when(bi == 0)
    def _():
        for cp in copies(0, 0):
            cp.start()

    @pl.when(bi + 1 < pl.num_programs(0))
    def _():
        for cp in copies(bi + 1, 1 - slot):
            cp.start()

    for cp in copies(bi, slot):
        cp.wait()

    m3 = lax.broadcasted_iota(jnp.int32, (t, n3), 0)
    c3 = lax.broadcasted_iota(jnp.int32, (t, n3), 1)
    mask3 = (c3 & (t - 1)) == m3
    ml = lax.broadcasted_iota(jnp.int32, (t, n_l), 0)
    cl = lax.broadcasted_iota(jnp.int32, (t, n_l), 1)
    dist_l = n_l + ml - cl
    mask1l = dist_l <= w1
    mask2l = (dist_l <= w2) & ((dist_l & (d2 - 1)) == 0)
    mn = lax.broadcasted_iota(jnp.int32, (t, t), 0)
    cn = lax.broadcasted_iota(jnp.int32, (t, t), 1)
    dist_n = mn - cn
    mask1n = dist_n >= 0
    mask2n = (dist_n >= 0) & ((dist_n & (d2 - 1)) == 0)
    mask3n = dist_n == 0

    for h in range(SWA_HEADS):
        hs = slice(h * SWA_HD, (h + 1) * SWA_HD)
        qh = _rope(q_ref[:, hs], cosf, sinf) * (SWA_HD ** -0.5)
        k_new = _rope(k_ref[:, hs], cosf, sinf)
        kr_ref[:, hs] = k_new
        v_new = v_ref[:, hs]
        k3 = kf[slot, h].reshape(n3, SWA_HD)
        v3 = vf[slot, h].reshape(n3, SWA_HD)
        kl = kn[slot, h].reshape(n_l, SWA_HD)
        vl = vn[slot, h].reshape(n_l, SWA_HD)
        s3 = _mm_nt(qh, k3)
        s_l = _mm_nt(qh, kl)
        s_n = _mm_nt(qh, k_new)

        def branch(parts):
            mx = None
            for sc, mask in parts:
                cur = jnp.max(jnp.where(mask, sc, NEG_BIG), axis=-1, keepdims=True)
                mx = cur if mx is None else jnp.maximum(mx, cur)
            ps = [jnp.where(mask, jnp.exp(jnp.where(mask, sc, NEG_BIG) - mx), 0.0) for sc, mask in parts]
            den = sum(jnp.sum(p, axis=-1, keepdims=True) for p in ps)
            return mx, ps, den

        mx1, (p1l, p1n), den1 = branch([(s_l, mask1l), (s_n, mask1n)])
        mx2, (p2l, p2n), den2 = branch([(s_l, mask2l), (s_n, mask2n)])
        mx3, (p3c, p3n), den3 = branch([(s3, mask3), (s_n, mask3n)])
        mx = jnp.maximum(jnp.maximum(mx1, mx2), mx3)
        wt1, wt2, wt3 = jnp.exp(mx1 - mx), jnp.exp(mx2 - mx), jnp.exp(mx3 - mx)
        num = (_mm(p1l * wt1 + p2l * wt2, vl) + _mm(p3c * wt3, v3)
               + _mm(p1n * wt1 + p2n * wt2 + p3n * wt3, v_new))
        den = den1 * wt1 + den2 * wt2 + den3 * wt3
        o_ref[:, hs] = (num / den).astype(o_ref.dtype)


def _swa_sample(proj, cache_k, cache_v, layer, b, t, past):
    depth, _, l_cache, nh, hd = cache_k.shape
    assert l_cache == past and nh * hd == SWA_W
    (_, _), (w2, _), (_, d3) = DILATIONS
    cosf, sinf = _rope_tables(past, t)
    assert l_cache % d3 == 0 and w2 % d3 == 0
    ck6 = cache_k.reshape(depth, b, l_cache // d3, d3, nh, hd)
    cv6 = cache_v.reshape(depth, b, l_cache // d3, d3, nh, hd)
    hbm = pl.BlockSpec(memory_space=pl.ANY)
    far_buf = pltpu.VMEM((2, nh, l_cache // d3, t, hd), F32)
    near_buf = pltpu.VMEM((2, nh, w2 // d3, d3, hd), F32)
    return pl.pallas_call(
        functools.partial(_swa_sample_kernel, layer=layer, t=t, past=past),
        grid=(b,),
        in_specs=[pl.BlockSpec((t, SWA_W), lambda bi: (bi, PK_AQ // SWA_W)),
                  pl.BlockSpec((t, SWA_W), lambda bi: (bi, PK_AK // SWA_W)),
                  pl.BlockSpec((t, SWA_W), lambda bi: (bi, PK_AV // SWA_W)),
                  pl.BlockSpec((t, LANES), lambda bi: (0, 0)),
                  pl.BlockSpec((t, LANES), lambda bi: (0, 0)),
                  hbm, hbm],
        out_specs=[pl.BlockSpec((t, SWA_W), lambda bi: (bi, 0)),
                   pl.BlockSpec((t, SWA_W), lambda bi: (bi, 0))],
        out_shape=[jax.ShapeDtypeStruct((b * t, SWA_W), BF16),
                   jax.ShapeDtypeStruct((b * t, SWA_W), F32)],
        scratch_shapes=[far_buf, far_buf, near_buf, near_buf, pltpu.SemaphoreType.DMA((2, 4, nh))],
        compiler_params=_cparams(("arbitrary",)),
        name="swa_sample",
    )(proj, proj, proj, cosf, sinf, ck6, cv6)


def _outproj_kernel(oa_ref, ob_ref, oc_ref, w_ref, x_ref, gate_ref, gpost_ref, gpre_ref, sh_ref, sc_ref,
                    xo_ref, h2_ref):
    mix = (jnp.dot(oa_ref[...], w_ref[0:GDN_W, :], preferred_element_type=F32)
           + jnp.dot(ob_ref[...], w_ref[GDN_W:GDN_W + SSM_W, :], preferred_element_type=F32)
           + jnp.dot(oc_ref[...], w_ref[GDN_W + SSM_W:, :], preferred_element_type=F32))
    y = _rms(mix, gpost_ref[...])
    x = x_ref[...] + gate_ref[...] * y.reshape(x_ref.shape)
    xo_ref[...] = x
    h2 = _rms(x, gpre_ref[...]) * (1.0 + sc_ref[...]) + sh_ref[...]
    h2_ref[...] = h2.reshape(h2_ref.shape).astype(h2_ref.dtype)


def _outproj(oa, ob, oc, w, x, mod, gpost, gpre, layer, bb, tt):
    b, t, d = x.shape
    nt = t // tt
    rows = bb * tt
    rmap = lambda bi, ti: (bi * nt + ti, 0)
    mspec = lambda k: pl.BlockSpec((None, bb, 1, d), lambda bi, ti: (layer, bi, 0, k))
    gspec = pl.BlockSpec((None, 1, d), lambda bi, ti: (layer, 0, 0))
    return pl.pallas_call(
        _outproj_kernel,
        grid=(b // bb, nt),
        in_specs=[pl.BlockSpec((rows, GDN_W), rmap), pl.BlockSpec((rows, SSM_W), rmap),
                  pl.BlockSpec((rows, SWA_W), rmap),
                  pl.BlockSpec((None, d, d), lambda bi, ti: (layer, 0, 0)),
                  pl.BlockSpec((bb, tt, d), lambda bi, ti: (bi, ti, 0)),
                  mspec(2), gspec, gspec, mspec(3), mspec(4)],
        out_specs=[pl.BlockSpec((bb, tt, d), lambda bi, ti: (bi, ti, 0)),
                   pl.BlockSpec((rows, d), rmap)],
        out_shape=[jax.ShapeDtypeStruct((b, t, d), F32), jax.ShapeDtypeStruct((b * t, d), BF16)],
        compiler_params=_cparams(("arbitrary", "arbitrary")),
        name="outproj",
    )(oa, ob, oc, w, x, mod, gpost, gpre, mod, mod)


def _ffn_up_kernel(*refs, tt, zero_init):
    if zero_init:
        (h_ref, wg_ref, wu_ref, cwg_ref, cwu_ref, cbg_ref, cbu_ref, act_ref, nsg_ref, nsu_ref, csg, csu) = refs
        stg_ref = stu_ref = None
    else:
        (h_ref, wg_ref, wu_ref, cwg_ref, cwu_ref, cbg_ref, cbu_ref, stg_ref, stu_ref,
         act_ref, nsg_ref, nsu_ref, csg, csu) = refs
    first = pl.program_id(2) == 0
    lo = SUBLANES - (FFN_CONV - 1)
    h = h_ref[...]

    def half(w_ref, cw_ref, cb_ref, st_ref, ns_ref, cs):
        bb = cs.shape[0]

        @pl.when(first)
        def _():
            if st_ref is None:
                cs[:, lo:SUBLANES, :] = jnp.zeros((bb, FFN_CONV - 1, cs.shape[2]), F32)
            else:
                cs[:, lo:SUBLANES, :] = st_ref[...]

        @pl.when(jnp.logical_not(first))
        def _():
            cs[:, lo:SUBLANES, :] = cs[:, tt + lo:tt + SUBLANES, :]

        up = jnp.dot(h, w_ref[...], preferred_element_type=F32)
        cs[:, SUBLANES:SUBLANES + tt, :] = up.reshape(bb, tt, up.shape[-1])
        y = cs[:, lo:lo + tt, :] * cw_ref[0:1, :]
        for j in range(1, FFN_CONV):
            y = y + cs[:, lo + j:lo + j + tt, :] * cw_ref[j:j + 1, :]
        ns_ref[...] = cs[:, tt + lo:tt + SUBLANES, :]
        return y + cb_ref[...]

    yg = half(wg_ref, cwg_ref, cbg_ref, stg_ref, nsg_ref, csg)
    yu = half(wu_ref, cwu_ref, cbu_ref, stu_ref, nsu_ref, csu)
    act = _silu(yg) * yu
    act_ref[...] = act.reshape(act_ref.shape).astype(act_ref.dtype)


def _ffn_up(h2, state, w_up, conv_w, conv_b, layer, b, t, bb, tt):
    zero_init = state is None
    d = h2.shape[1]
    tn = 512
    nh = D_FF // tn
    nt = t // tt
    rows = bb * tt
    depth = conv_w.shape[0]
    cb = conv_b.reshape(depth, 1, -1)
    in_specs = [pl.BlockSpec((rows, d), lambda bi, j, ti: (bi * nt + ti, 0)),
                pl.BlockSpec((None, d, tn), lambda bi, j, ti: (layer, 0, j)),
                pl.BlockSpec((None, d, tn), lambda bi, j, ti: (layer, 0, nh + j)),
                pl.BlockSpec((None, FFN_CONV, tn), lambda bi, j, ti: (layer, 0, j)),
                pl.BlockSpec((None, FFN_CONV, tn), lambda bi, j, ti: (layer, 0, nh + j)),
                pl.BlockSpec((None, 1, tn), lambda bi, j, ti: (layer, 0, j)),
                pl.BlockSpec((None, 1, tn), lambda bi, j, ti: (layer, 0, nh + j))]
    args = [h2, w_up, w_up, conv_w, conv_w, cb, cb]
    if not zero_init:
        in_specs += [pl.BlockSpec((None, bb, FFN_CONV - 1, tn), lambda bi, j, ti: (layer, bi, 0, j)),
                     pl.BlockSpec((None, bb, FFN_CONV - 1, tn), lambda bi, j, ti: (layer, bi, 0, nh + j))]
        args += [state, state]
    ns_spec = pl.BlockSpec((bb, FFN_CONV - 1, tn), lambda bi, j, ti: (bi, 0, j))
    act, nsg, nsu = pl.pallas_call(
        functools.partial(_ffn_up_kernel, tt=tt, zero_init=zero_init),
        grid=(b // bb, nh, nt),
        in_specs=in_specs,
        out_specs=[pl.BlockSpec((rows, tn), lambda bi, j, ti: (bi * nt + ti, j)), ns_spec, ns_spec],
        out_shape=[jax.ShapeDtypeStruct((b * t, D_FF), BF16),
                   jax.ShapeDtypeStruct((b, FFN_CONV - 1, D_FF), F32),
                   jax.ShapeDtypeStruct((b, FFN_CONV - 1, D_FF), F32)],
        scratch_shapes=[pltpu.VMEM((bb, tt + SUBLANES, tn), F32)] * 2,
        compiler_params=_cparams(("arbitrary", "arbitrary", "arbitrary")),
        name="ffn_up",
    )(*args)
    return act, jnp.concatenate([nsg, nsu], axis=-1)


def _ffn_down_kernel(a_ref, w_ref, x_ref, gate_ref, g_ref, o_ref, acc):
    k = pl.program_id(2)

    @pl.when(k == 0)
    def _():
        acc[...] = jnp.zeros_like(acc)

    acc[...] += jnp.dot(a_ref[...], w_ref[...], preferred_element_type=F32)

    @pl.when(k == pl.num_programs(2) - 1)
    def _():
        y = _rms(acc[...], g_ref[...])
        o_ref[...] = x_ref[...] + gate_ref[...] * y.reshape(x_ref.shape)


def _ffn_down(act, w_down, x, mod, g, layer, bb, tt):
    b, t, d = x.shape
    nt = t // tt
    rows = bb * tt
    tk = 1408
    return pl.pallas_call(
        _ffn_down_kernel,
        grid=(b // bb, nt, D_FF // tk),
        in_specs=[pl.BlockSpec((rows, tk), lambda bi, ti, k: (bi * nt + ti, k)),
                  pl.BlockSpec((None, tk, d), lambda bi, ti, k: (layer, k, 0)),
                  pl.BlockSpec((bb, tt, d), lambda bi, ti, k: (bi, ti, 0)),
                  pl.BlockSpec((None, bb, 1, d), lambda bi, ti, k: (layer, bi, 0, 5)),
                  pl.BlockSpec((None, 1, d), lambda bi, ti, k: (layer, 0, 0))],
        out_specs=pl.BlockSpec((bb, tt, d), lambda bi, ti, k: (bi, ti, 0)),
        out_shape=jax.ShapeDtypeStruct((b, t, d), F32),
        scratch_shapes=[pltpu.VMEM((rows, d), F32)],
        compiler_params=_cparams(("arbitrary", "arbitrary", "arbitrary")),
        name="ffn_down",
    )(act, w_down, x, mod, g)


def _pack_w_in(w_in):
    depth, d, _ = w_in.shape
    o_gb = 4 * GDN_W
    o_sz = o_gb + 2 * GDN_HEADS
    o_dt = o_sz + SSM_W + SSM_CONV_CH
    o_aq = o_dt + SSM_HEADS
    pad = jnp.zeros((depth, d, LANES - 2 * GDN_HEADS - SSM_HEADS), w_in.dtype)
    packed = jnp.concatenate([w_in[..., :o_gb], w_in[..., o_sz:o_dt], w_in[..., o_aq:],
                              w_in[..., o_gb:o_sz], w_in[..., o_dt:o_aq], pad], axis=-1)
    assert packed.shape[-1] == PK_COLS
    return packed.astype(BF16)


def _layer(x, mod, layer, states, wts, past, bb, tt):
    b, t, d = x.shape
    if states is None:
        gdn_conv = gdn_s = ssm_conv = ssm_h = cache_k = cache_v = ffn_conv = None
    else:
        gdn_conv, gdn_s, ssm_conv, ssm_h, cache_k, cache_v, ffn_conv = states
    proj = _inproj(x, mod, wts["g_pre_mix"], wts["w_in"], layer, bb, tt)
    gdn_w = (wts["gdn_conv_w"], wts["gdn_a_log"], wts["gdn_dt_bias"], wts["gdn_norm_g"])
    if states is None:
        o_a, new_gdn_s = _gdn_seq(proj, *gdn_w, layer, b, t)
    else:
        o_a, new_gdn_s = _gdn_step(proj, gdn_conv, gdn_s, *gdn_w, layer, b, t)
    o_b, new_ssm_h = _ssd(proj, ssm_conv, ssm_h, wts["ssm_conv_w"], wts["ssm_conv_b"], wts["ssm_dt_bias"],
                          wts["ssm_a_log"], wts["ssm_d"], wts["ssm_norm_g"], layer, b, t)
    if states is None:
        o_c, k_rot = _swa_prompt(proj, b, t)
    else:
        o_c, k_rot = _swa_sample(proj, cache_k, cache_v, layer, b, t, past)
    x_mid, h2 = _outproj(o_a, o_b, o_c, wts["w_out"], x, mod, wts["g_post_mix"], wts["g_pre_ffn"], layer, bb, tt)
    act, new_ffn_conv = _ffn_up(h2, ffn_conv, wts["w_up"], wts["ffn_conv_w"], wts["ffn_conv_b"], layer, b, t, bb, tt)
    x_out = _ffn_down(act, wts["w_down"], x_mid, mod, wts["g_post_ffn"], layer, bb, tt)

    proj3 = proj.reshape(b, t, PK_COLS)
    keep = min(W_MAX, t)
    new_gdn_conv = proj3[:, t - (GDN_CONV - 1):, PK_GQ:PK_GQ + 3 * GDN_W]
    new_ssm_conv = proj3[:, t - (SSM_CONV - 1):, PK_SX:PK_SX + SSM_CONV_CH]
    new_k = k_rot.reshape(b, t, SWA_HEADS, SWA_HD)[:, t - keep:]
    new_v = proj3[:, t - keep:, PK_AV:PK_AV + SWA_W].reshape(b, keep, SWA_HEADS, SWA_HD)
    return x_out, (new_gdn_conv, new_gdn_s, new_ssm_conv, new_ssm_h, new_k, new_v, new_ffn_conv)


def kernel(x_prompt, x_sample, c_prompt, c_sample, state_gdn_conv, state_gdn, state_ssm_conv, state_ssm, cache_k, cache_v, state_ffn_conv, w_ada, b_ada, g_pre_mix, g_post_mix, g_pre_ffn, g_post_ffn, w_in, gdn_conv_w, gdn_a_log, gdn_dt_bias, gdn_norm_g, ssm_conv_w, ssm_conv_b, ssm_dt_bias, ssm_a_log, ssm_d, ssm_norm_g, w_out, w_up, ffn_conv_w, ffn_conv_b, w_down):
    depth = w_ada.shape[0]
    bp, tp, d = x_prompt.shape
    bs, ts, _ = x_sample.shape
    past = cache_k.shape[2]
    assert tp >= GDN_CONV and ts >= GDN_CONV and ts % SUBLANES == 0

    vec = lambda a: a.reshape(depth, 1, a.shape[-1])
    wts = dict(
        g_pre_mix=vec(g_pre_mix), g_post_mix=vec(g_post_mix), g_pre_ffn=vec(g_pre_ffn), g_post_ffn=vec(g_post_ffn),
        w_in=_pack_w_in(w_in), w_out=w_out.astype(BF16), w_up=w_up.astype(BF16), w_down=w_down.astype(BF16),
        gdn_conv_w=gdn_conv_w, gdn_a_log=gdn_a_log, gdn_dt_bias=gdn_dt_bias, gdn_norm_g=gdn_norm_g,
        ssm_conv_w=ssm_conv_w, ssm_conv_b=ssm_conv_b, ssm_dt_bias=ssm_dt_bias, ssm_a_log=ssm_a_log, ssm_d=ssm_d,
        ssm_norm_g=ssm_norm_g, ffn_conv_w=ffn_conv_w, ffn_conv_b=ffn_conv_b)

    mod = _ada(jnp.concatenate([c_prompt, c_sample], axis=0), w_ada, b_ada)
    mod_p = mod[:, :bp].reshape(depth, bp, 1, 6 * d)
    mod_s = mod[:, bp:].reshape(depth, bs, 1, 6 * d)
    s_states = (state_gdn_conv, state_gdn, state_ssm_conv, state_ssm, cache_k, cache_v, state_ffn_conv)

    tt_p = math.gcd(tp, 512)
    bb_s = math.gcd(bs, 512 // ts)
    y_p, y_s = x_prompt, x_sample
    p_rows, s_rows = [], []
    for layer in range(depth):
        y_p, outs_p = _layer(y_p, mod_p, layer, None, wts, 0, 1, tt_p)
        y_s, outs_s = _layer(y_s, mod_s, layer, s_states, wts, past, bb_s, ts)
        p_rows.append(outs_p)
        s_rows.append(outs_s)
    p_out = [jnp.stack(a) for a in zip(*p_rows)]
    s_out = [jnp.stack(a) for a in zip(*s_rows)]
    return (y_p, y_s, *p_out, *s_out)
```

```python
import functools
import math

import jax
import jax.numpy as jnp
import numpy as np
from jax import lax
from jax.experimental import pallas as pl
from jax.experimental.pallas import tpu as pltpu

F32 = jnp.float32
BF16 = jnp.bfloat16

D_MODEL = 2048
MIX_UNIT = D_MODEL // 8
GDN_W = 3 * MIX_UNIT
SSM_W = 2 * MIX_UNIT
SWA_W = 3 * MIX_UNIT
GDN_DK = 128
GDN_DV = 128
GDN_HEADS = GDN_W // GDN_DV
GDN_CONV = 4
GDN_CHUNK = 64
SSM_HEADDIM = 64
SSM_HEADS = SSM_W // SSM_HEADDIM
SSM_GROUPS = 2
SSM_STATE = 128
SSM_CONV = 4
SSM_CHUNK = 64
SSM_CONV_CH = SSM_W + 2 * SSM_GROUPS * SSM_STATE
SWA_HD = 128
SWA_HEADS = SWA_W // SWA_HD
DILATIONS = ((128, 1), (512, 4), (2048, 16))
W_MAX = 2048
ROPE_THETA = 10000.0
D_FF = 11 * D_MODEL // 4
FFN_CONV = 3
NORM_EPS = 1e-6

LANES = 128
SUBLANES = 8
VMEM_LIMIT = 56 * 1024 * 1024

PK_GQ = 0
PK_GK = PK_GQ + GDN_W
PK_GV = PK_GK + GDN_W
PK_GZ = PK_GV + GDN_W
PK_SZ = PK_GZ + GDN_W
PK_SX = PK_SZ + SSM_W
PK_SBC = PK_SX + SSM_W
PK_AQ = PK_SX + SSM_CONV_CH
PK_AK = PK_AQ + SWA_W
PK_AV = PK_AK + SWA_W
PK_SM = PK_AV + SWA_W
PK_COLS = PK_SM + LANES
SM_BETA = 0
SM_A = GDN_HEADS
SM_DT = 2 * GDN_HEADS
NEG_BIG = -1e30


def _cparams(sem):
    return pltpu.CompilerParams(dimension_semantics=sem, vmem_limit_bytes=VMEM_LIMIT)


def _sigmoid(x):
    return 1.0 / (1.0 + jnp.exp(-x))


def _silu(x):
    return x * _sigmoid(x)


def _softplus(x):
    return jnp.maximum(x, 0.0) + jnp.log1p(jnp.exp(-jnp.abs(x)))


def _mm(a, b):
    return jnp.dot(a.astype(BF16), b.astype(BF16), preferred_element_type=F32)


def _mm_nt(a, b):
    return lax.dot_general(a.astype(BF16), b.astype(BF16), (((1,), (1,)), ((), ())), preferred_element_type=F32)


def _mm_hi(a, b):
    return jnp.dot(a, b, preferred_element_type=F32, precision=lax.Precision.HIGHEST)


def _rms(x, g):
    return x * lax.rsqrt(jnp.mean(x * x, axis=-1, keepdims=True) + NORM_EPS) * g


def _ada_kernel(c_ref, w_ref, b_ref, o_ref):
    a = _silu(c_ref[...]).astype(BF16)
    o_ref[...] = jnp.dot(a, w_ref[...].astype(BF16), preferred_element_type=F32) + b_ref[...]


def _ada(c_all, w_ada, b_ada):
    depth, d, n = w_ada.shape
    r = c_all.shape[0]
    tn = 1024
    return pl.pallas_call(
        _ada_kernel,
        grid=(depth, n // tn),
        in_specs=[pl.BlockSpec((r, d), lambda l, j: (0, 0)),
                  pl.BlockSpec((None, d, tn), lambda l, j: (l, 0, j)),
                  pl.BlockSpec((None, 1, tn), lambda l, j: (l, 0, j))],
        out_specs=pl.BlockSpec((None, r, tn), lambda l, j: (l, 0, j)),
        out_shape=jax.ShapeDtypeStruct((depth, r, n), F32),
        compiler_params=_cparams(("arbitrary", "arbitrary")),
        name="ada",
    )(c_all, w_ada, b_ada.reshape(depth, 1, n))


def _inproj_kernel(x_ref, sh_ref, sc_ref, g_ref, w_ref, o_ref, h_scr):
    @pl.when(pl.program_id(2) == 0)
    def _():
        h = _rms(x_ref[...], g_ref[...]) * (1.0 + sc_ref[...]) + sh_ref[...]
        h_scr[...] = h.reshape(h_scr.shape).astype(BF16)

    o_ref[...] = jnp.dot(h_scr[...], w_ref[...], preferred_element_type=F32)


def _inproj(x, mod, g, w, layer, bb, tt):
    b, t, d = x.shape
    n = w.shape[-1]
    tn = 1408
    nt = t // tt
    return pl.pallas_call(
        _inproj_kernel,
        grid=(b // bb, nt, n // tn),
        in_specs=[pl.BlockSpec((bb, tt, d), lambda bi, ti, j: (bi, ti, 0)),
                  pl.BlockSpec((None, bb, 1, d), lambda bi, ti, j: (layer, bi, 0, 0)),
                  pl.BlockSpec((None, bb, 1, d), lambda bi, ti, j: (layer, bi, 0, 1)),
                  pl.BlockSpec((None, 1, d), lambda bi, ti, j: (layer, 0, 0)),
                  pl.BlockSpec((None, d, tn), lambda bi, ti, j: (layer, 0, j))],
        out_specs=pl.BlockSpec((bb * tt, tn), lambda bi, ti, j: (bi * nt + ti, j)),
        out_shape=jax.ShapeDtypeStruct((b * t, n), F32),
        scratch_shapes=[pltpu.VMEM((bb * tt, d), BF16)],
        compiler_params=_cparams(("arbitrary", "arbitrary", "arbitrary")),
        name="inproj",
    )(x, mod, mod, g, w)


def _conv_from_scratch(xs, w_ref, width, t):
    lo = SUBLANES - (width - 1)
    y = xs[lo:lo + t, :] * w_ref[0:1, :]
    for j in range(1, width):
        y = y + xs[lo + j:lo + j + t, :] * w_ref[j:j + 1, :]
    return y


def _conv_stage(xs, x_new, hist_ref, first, width, t):
    lo = SUBLANES - (width - 1)

    @pl.when(first)
    def _():
        if hist_ref is None:
            xs[lo:SUBLANES, :] = jnp.zeros((width - 1, xs.shape[1]), F32)
        else:
            xs[lo:SUBLANES, :] = hist_ref[...]

    @pl.when(jnp.logical_not(first))
    def _():
        xs[lo:SUBLANES, :] = xs[t + lo:t + SUBLANES, :]

    xs[SUBLANES:SUBLANES + t, :] = x_new


def _tri_masks(c):
    row = lax.broadcasted_iota(jnp.int32, (c, c), 0)
    col = lax.broadcasted_iota(jnp.int32, (c, c), 1)
    return row == col, row >= col, row > col, row <= col


def _bdot(a, b, ca, cb):
    return lax.dot_general(a, b, (((ca,), (cb,)), ((0,), (0,))), preferred_element_type=F32)


def _bmm(a, b):
    return _bdot(a.astype(BF16), b.astype(BF16), 2, 1)


def _bmm_nt(a, b):
    return _bdot(a.astype(BF16), b.astype(BF16), 2, 2)


def _bmm_tn(a, b):
    return _bdot(a.astype(BF16), b.astype(BF16), 1, 1)


def _split_bf16(a):
    hi = a.astype(BF16)
    return hi, (a - hi.astype(F32)).astype(BF16)


def _bmm3(a, b):
    ah, al = _split_bf16(a)
    bh, bl = _split_bf16(b)
    return _bdot(ah, bh, 2, 1) + _bdot(ah, bl, 2, 1) + _bdot(al, bh, 2, 1)


def _unit_lower_inverse(a, eye_f, c):
    x = eye_f - a
    p = _bmm3(a, a)
    n = 2
    while True:
        x = x + _bmm3(x, p)
        n *= 2
        if n >= c:
            return x
        p = _bmm3(p, p)


def _gdn_prep(q, k, v, beta, g, c):
    eye, causal, strict, upper = _tri_masks(c)
    g_row = jnp.sum(jnp.where(eye, g, 0.0), axis=1, keepdims=True)
    cg_col = jnp.sum(jnp.where(causal, g_row, 0.0), axis=2, keepdims=True)
    cg_row = jnp.sum(jnp.where(upper, g, 0.0), axis=1, keepdims=True)
    gam = jnp.where(causal, jnp.exp(jnp.where(causal, cg_col - cg_row, 0.0)), 0.0)
    kb = k * beta
    amat = jnp.where(strict, _bmm_nt(kb, k) * gam, 0.0)
    tinv = _unit_lower_inverse(amat, eye.astype(F32), c)
    ecg = jnp.exp(cg_col)
    u = _bmm3(tinv, v * beta)
    w = _bmm3(tinv, kb * ecg)
    qk = _bmm_nt(q, k) * gam
    cg_last = cg_col[:, c - 1:c, :]
    kd = k * jnp.exp(cg_last - cg_col)
    return u, w, qk, q * ecg, kd, jnp.exp(cg_last)


def _gdn_gates(b_raw, a_raw, a_log, dt_bias):
    a_neg = -jnp.exp(jnp.full((1, 1), a_log, F32))
    return _sigmoid(b_raw), a_neg * _softplus(a_raw + dt_bias)


def _l2norm(x):
    return x * lax.rsqrt(jnp.sum(x * x, axis=-1, keepdims=True) + NORM_EPS)


def _gdn_seq_kernel(alog_ref, dtb_ref, q_ref, k_ref, v_ref, z_ref, sm_ref, wq_ref, wk_ref, wv_ref, ng_ref,
                    o_ref, sfin_ref, xq, xk, xv, s_scr, *, c, nc):
    h = pl.program_id(1)
    tb = c * nc
    first = pl.program_id(2) == 0

    @pl.when(first)
    def _():
        s_scr[...] = jnp.zeros_like(s_scr)

    _conv_stage(xq, q_ref[...], None, first, GDN_CONV, tb)
    _conv_stage(xk, k_ref[...], None, first, GDN_CONV, tb)
    _conv_stage(xv, v_ref[...], None, first, GDN_CONV, tb)
    q = _l2norm(_silu(_conv_from_scratch(xq, wq_ref, GDN_CONV, tb))) * (GDN_DK ** -0.5)
    k = _l2norm(_silu(_conv_from_scratch(xk, wk_ref, GDN_CONV, tb)))
    v = _silu(_conv_from_scratch(xv, wv_ref, GDN_CONV, tb))

    sm = sm_ref[...]
    lane = lax.broadcasted_iota(jnp.int32, sm.shape, 1)
    b_raw = jnp.sum(jnp.where(lane == SM_BETA + h, sm, 0.0), axis=1, keepdims=True)
    a_raw = jnp.sum(jnp.where(lane == SM_A + h, sm, 0.0), axis=1, keepdims=True)
    beta, g = _gdn_gates(b_raw, a_raw, alog_ref[h], dtb_ref[h])

    chunks = lambda a: a.reshape(nc, c, a.shape[-1])
    u, w, qk, qg, kd, gl = _gdn_prep(chunks(q), chunks(k), chunks(v), chunks(beta), chunks(g), c)

    ng = ng_ref[...]
    s = s_scr[...]
    for ci in range(nc):
        sl = slice(ci * c, (ci + 1) * c)
        ws = _mm(jnp.concatenate([w[ci], qg[ci]], axis=0), s)
        vnew = u[ci] - ws[:c]
        r = _mm(jnp.concatenate([qk[ci], kd[ci].T], axis=0), vnew)
        s = s * gl[ci] + r[c:]
        o = _rms(ws[c:] + r[:c], ng) * _silu(z_ref[sl, :])
        o_ref[sl, :] = o.astype(o_ref.dtype)
    s_scr[...] = s
    sfin_ref[...] = s


def _gdn_seq(proj, conv_w, a_log, dt_bias, norm_g, layer, b, t):
    c = math.gcd(t, GDN_CHUNK)
    nc = math.gcd(t // c, 8)
    tb = c * nc
    nt = t // tb
    hq, hk, hv, hz = PK_GQ // LANES, PK_GK // LANES, PK_GV // LANES, PK_GZ // LANES
    smem = pl.BlockSpec(memory_space=pltpu.SMEM)
    cblk = lambda off: pl.BlockSpec((tb, LANES), lambda bi, h, i: (bi * nt + i, off + h))
    wblk = lambda off: pl.BlockSpec((None, GDN_CONV, LANES), lambda bi, h, i: (layer, 0, off + h))
    return pl.pallas_call(
        functools.partial(_gdn_seq_kernel, c=c, nc=nc),
        grid=(b, GDN_HEADS, nt),
        in_specs=[smem, smem, cblk(hq), cblk(hk), cblk(hv), cblk(hz),
                  pl.BlockSpec((tb, LANES), lambda bi, h, i: (bi * nt + i, PK_SM // LANES)),
                  wblk(0), wblk(GDN_HEADS), wblk(2 * GDN_HEADS),
                  pl.BlockSpec((None, 1, LANES), lambda bi, h, i: (layer, 0, 0))],
        out_specs=[cblk(0), pl.BlockSpec((None, None, GDN_DK, GDN_DV), lambda bi, h, i: (bi, h, 0, 0))],
        out_shape=[jax.ShapeDtypeStruct((b * t, GDN_W), BF16),
                   jax.ShapeDtypeStruct((b, GDN_HEADS, GDN_DK, GDN_DV), F32)],
        scratch_shapes=[pltpu.VMEM((tb + SUBLANES, LANES), F32)] * 3 + [pltpu.VMEM((GDN_DK, GDN_DV), F32)],
        compiler_params=_cparams(("arbitrary", "arbitrary", "arbitrary")),
        name="gdn_seq",
    )(a_log[layer], dt_bias[layer], proj, proj, proj, proj, proj, conv_w, conv_w, conv_w,
      norm_g.reshape(norm_g.shape[0], 1, LANES))


def _gdn_step_kernel(alog_ref, dtb_ref, x_ref, z_ref, sm_ref, w_ref, ng_ref, hist_ref, s0_ref,
                     o_ref, sfin_ref, cs, *, bb, t):
    lo = SUBLANES - (GDN_CONV - 1)
    cs[:, lo:SUBLANES, :] = hist_ref[...]
    cs[:, SUBLANES:SUBLANES + t, :] = x_ref[...].reshape(bb, t, 3 * GDN_W)
    y = cs[:, lo:lo + t, :] * w_ref[0:1, :]
    for j in range(1, GDN_CONV):
        y = y + cs[:, lo + j:lo + j + t, :] * w_ref[j:j + 1, :]
    y = _silu(y)
    sm = sm_ref[...].reshape(bb, t, LANES)
    ng = ng_ref[...]
    for h in range(GDN_HEADS):
        hs = slice(h * GDN_DK, (h + 1) * GDN_DK)
        q = _l2norm(y[:, :, hs]) * (GDN_DK ** -0.5)
        k = _l2norm(y[:, :, GDN_W + h * GDN_DK:GDN_W + (h + 1) * GDN_DK])
        v = y[:, :, 2 * GDN_W + h * GDN_DV:2 * GDN_W + (h + 1) * GDN_DV]
        beta, g = _gdn_gates(sm[:, :, SM_BETA + h:SM_BETA + h + 1], sm[:, :, SM_A + h:SM_A + h + 1],
                             alog_ref[h], dtb_ref[h])
        u, w, qk, qg, kd, gl = _gdn_prep(q, k, v, beta, g, t)
        s = s0_ref[:, h]
        vnew = u - _bmm(w, s)
        o = _bmm(qg, s) + _bmm(qk, vnew)
        sfin_ref[:, h] = s * gl + _bmm_tn(kd, vnew)
        o = _rms(o, ng) * _silu(z_ref[:, hs].reshape(bb, t, GDN_DV))
        o_ref[:, hs] = o.reshape(bb * t, GDN_DV).astype(o_ref.dtype)


def _gdn_step(proj, conv_buf, s0, conv_w, a_log, dt_bias, norm_g, layer, b, t):
    assert t % SUBLANES == 0 and GDN_CHUNK % t == 0 and t >= GDN_CONV - 1
    bb = math.gcd(b, 8)
    rows = bb * t
    smem = pl.BlockSpec(memory_space=pltpu.SMEM)
    return pl.pallas_call(
        functools.partial(_gdn_step_kernel, bb=bb, t=t),
        grid=(b // bb,),
        in_specs=[smem, smem,
                  pl.BlockSpec((rows, 3 * GDN_W), lambda bi: (bi, PK_GQ // (3 * GDN_W))),
                  pl.BlockSpec((rows, GDN_W), lambda bi: (bi, PK_GZ // GDN_W)),
                  pl.BlockSpec((rows, LANES), lambda bi: (bi, PK_SM // LANES)),
                  pl.BlockSpec((None, GDN_CONV, 3 * GDN_W), lambda bi: (layer, 0, 0)),
                  pl.BlockSpec((None, 1, LANES), lambda bi: (layer, 0, 0)),
                  pl.BlockSpec((None, bb, GDN_CONV - 1, 3 * GDN_W), lambda bi: (layer, bi, 0, 0)),
                  pl.BlockSpec((None, bb, GDN_HEADS, GDN_DK, GDN_DV), lambda bi: (layer, bi, 0, 0, 0))],
        out_specs=[pl.BlockSpec((rows, GDN_W), lambda bi: (bi, 0)),
                   pl.BlockSpec((bb, GDN_HEADS, GDN_DK, GDN_DV), lambda bi: (bi, 0, 0, 0))],
        out_shape=[jax.ShapeDtypeStruct((b * t, GDN_W), BF16),
                   jax.ShapeDtypeStruct((b, GDN_HEADS, GDN_DK, GDN_DV), F32)],
        scratch_shapes=[pltpu.VMEM((bb, t + SUBLANES, 3 * GDN_W), F32)],
        compiler_params=_cparams(("arbitrary",)),
        name="gdn_step",
    )(a_log[layer], dt_bias[layer], proj, proj, proj, conv_w, norm_g.reshape(norm_g.shape[0], 1, LANES),
      conv_buf, s0)


def _ssd_kernel(*refs, c, nc, bb, zero_init):
    if zero_init:
        (z_ref, x_ref, bc_ref, sm_ref, wx_ref, wbc_ref, bx_ref, bbc_ref, dtb_ref, alog_ref, dsk_ref, ng_ref,
         y_ref, hfin_ref, xs, xbc, h_scr) = refs
        cx_ref = cbc_ref = h0_ref = None
    else:
        (z_ref, x_ref, bc_ref, sm_ref, wx_ref, wbc_ref, bx_ref, bbc_ref, dtb_ref, alog_ref, dsk_ref, ng_ref,
         cx_ref, cbc_ref, h0_ref, y_ref, hfin_ref, xs, xbc, h_scr) = refs
    tb = c * nc
    first = pl.program_id(1) == 0
    lo = SUBLANES - (SSM_CONV - 1)

    @pl.when(first)
    def _():
        if zero_init:
            h_scr[...] = jnp.zeros_like(h_scr)
        else:
            h_scr[...] = h0_ref[...]

    def conv(cs, new_ref, hist_ref, w_ref, b_ref):
        width = cs.shape[2]

        @pl.when(first)
        def _():
            if hist_ref is None:
                cs[:, lo:SUBLANES, :] = jnp.zeros((bb, SSM_CONV - 1, width), F32)
            else:
                cs[:, lo:SUBLANES, :] = hist_ref[...]

        @pl.when(jnp.logical_not(first))
        def _():
            cs[:, lo:SUBLANES, :] = cs[:, tb + lo:tb + SUBLANES, :]

        cs[:, SUBLANES:SUBLANES + tb, :] = new_ref[...].reshape(bb, tb, width)
        y = cs[:, lo:lo + tb, :] * w_ref[0:1, :]
        for j in range(1, SSM_CONV):
            y = y + cs[:, lo + j:lo + j + tb, :] * w_ref[j:j + 1, :]
        return _silu(y + b_ref[...])

    xv = conv(xs, x_ref, cx_ref, wx_ref, bx_ref)
    bcv = conv(xbc, bc_ref, cbc_ref, wbc_ref, bbc_ref)
    gs = SSM_GROUPS * SSM_STATE

    dt_all = _softplus(sm_ref[...] + dtb_ref[...]).reshape(bb, tb, LANES)
    da_all = dt_all * (-jnp.exp(alog_ref[...]))
    _, causal, _, _ = _tri_masks(c)
    tril_f = causal.astype(F32)
    rep = SSM_HEADS // SSM_GROUPS
    gw = SSM_W // SSM_GROUPS
    ng = ng_ref[...]
    for bi in range(bb):
        for ci in range(nc):
            sl = slice(ci * c, (ci + 1) * c)
            rows = slice(bi * tb + ci * c, bi * tb + (ci + 1) * c)
            cg_all = _mm_hi(tril_f, da_all[bi, sl])
            cg_t = cg_all.T
            ys = []
            for grp in range(SSM_GROUPS):
                bm = bcv[bi, sl, grp * SSM_STATE:(grp + 1) * SSM_STATE]
                cm = bcv[bi, sl, gs + grp * SSM_STATE:gs + (grp + 1) * SSM_STATE]
                cb = _mm_nt(cm, bm)
                for hh in range(rep):
                    hd = grp * rep + hh
                    ln = SM_DT + hd
                    cg_col = cg_all[:, ln:ln + 1]
                    cg_row = cg_t[ln:ln + 1, :]
                    lmat = jnp.where(causal, jnp.exp(jnp.where(causal, cg_col - cg_row, 0.0)), 0.0)
                    xh = xv[bi, sl, hd * SSM_HEADDIM:(hd + 1) * SSM_HEADDIM]
                    xdt = xh * dt_all[bi, sl, ln:ln + 1]
                    cg_last = cg_col[c - 1:c, :]
                    hst = h_scr[bi, hd]
                    ys.append(_mm(cb * lmat, xdt) + _mm_nt(cm * jnp.exp(cg_col), hst))
                    h_scr[bi, hd] = hst * jnp.exp(cg_last) + _mm(xdt.T, bm * jnp.exp(cg_last - cg_col))
            yc = jnp.concatenate(ys, axis=-1) + dsk_ref[...] * xv[bi, sl]
            yc = yc * _silu(z_ref[rows, :])
            outs = [_rms(yc[:, gi * gw:(gi + 1) * gw], ng[:, gi * gw:(gi + 1) * gw]) for gi in range(SSM_GROUPS)]
            y_ref[rows, :] = jnp.concatenate(outs, axis=-1).astype(y_ref.dtype)

    hfin_ref[...] = h_scr[...]


def _lane_row(vals, offset):
    depth, n = vals.shape
    return jnp.pad(vals.astype(F32), ((0, 0), (offset, LANES - offset - n))).reshape(depth, 1, LANES)


def _ssd(proj, conv_buf, h0, conv_w, conv_b, dt_bias, a_log, d_skip, norm_g, layer, b, t):
    zero_init = h0 is None
    c = math.gcd(t, SSM_CHUNK)
    nc = max(1, min(4, t // c))
    tb = c * nc
    nt = t // tb
    bb = math.gcd(b, 8) if nt == 1 else 1
    rb = bb * tb
    depth = conv_w.shape[0]
    row = lambda bi, i: bi * nt + i
    wblk = SSM_W
    st_dims = (SSM_HEADS, SSM_HEADDIM, SSM_STATE)
    in_specs = [pl.BlockSpec((rb, wblk), lambda bi, i: (row(bi, i), PK_SZ // wblk)),
                pl.BlockSpec((rb, wblk), lambda bi, i: (row(bi, i), PK_SX // wblk)),
                pl.BlockSpec((rb, wblk), lambda bi, i: (row(bi, i), PK_SBC // wblk)),
                pl.BlockSpec((rb, LANES), lambda bi, i: (row(bi, i), PK_SM // LANES)),
                pl.BlockSpec((None, SSM_CONV, wblk), lambda bi, i: (layer, 0, 0)),
                pl.BlockSpec((None, SSM_CONV, wblk), lambda bi, i: (layer, 0, 1)),
                pl.BlockSpec((None, 1, wblk), lambda bi, i: (layer, 0, 0)),
                pl.BlockSpec((None, 1, wblk), lambda bi, i: (layer, 0, 1)),
                pl.BlockSpec((None, 1, LANES), lambda bi, i: (layer, 0, 0)),
                pl.BlockSpec((None, 1, LANES), lambda bi, i: (layer, 0, 0)),
                pl.BlockSpec((None, 1, SSM_W), lambda bi, i: (layer, 0, 0)),
                pl.BlockSpec((None, 1, SSM_W), lambda bi, i: (layer, 0, 0))]
    args = [proj, proj, proj, proj, conv_w, conv_w, conv_b.reshape(depth, 1, -1), conv_b.reshape(depth, 1, -1),
            _lane_row(dt_bias, SM_DT), _lane_row(a_log, SM_DT),
            jnp.repeat(d_skip.astype(F32), SSM_HEADDIM, axis=-1).reshape(depth, 1, SSM_W),
            norm_g.reshape(depth, 1, SSM_W)]
    if not zero_init:
        in_specs += [pl.BlockSpec((None, bb, SSM_CONV - 1, wblk), lambda bi, i: (layer, bi, 0, 0)),
                     pl.BlockSpec((None, bb, SSM_CONV - 1, wblk), lambda bi, i: (layer, bi, 0, 1)),
                     pl.BlockSpec((None, bb) + st_dims, lambda bi, i: (layer, bi, 0, 0, 0))]
        args += [conv_buf, conv_buf, jnp.swapaxes(h0, -1, -2)]
    y, h_fin_t = pl.pallas_call(
        functools.partial(_ssd_kernel, c=c, nc=nc, bb=bb, zero_init=zero_init),
        grid=(b // bb, nt),
        in_specs=in_specs,
        out_specs=[pl.BlockSpec((rb, SSM_W), lambda bi, i: (row(bi, i), 0)),
                   pl.BlockSpec((bb,) + st_dims, lambda bi, i: (bi, 0, 0, 0))],
        out_shape=[jax.ShapeDtypeStruct((b * t, SSM_W), BF16), jax.ShapeDtypeStruct((b,) + st_dims, F32)],
        scratch_shapes=[pltpu.VMEM((bb, tb + SUBLANES, wblk), F32)] * 2 + [pltpu.VMEM((bb,) + st_dims, F32)],
        compiler_params=_cparams(("arbitrary", "arbitrary")),
        name="ssd",
    )(*args)
    return y, jnp.swapaxes(h_fin_t, -1, -2)


def _rope_tables(pos0, t):
    half = SWA_HD // 2
    inv = ROPE_THETA ** (-jnp.arange(half, dtype=F32) / half)
    ang = (pos0 + jnp.arange(t, dtype=jnp.int32)).astype(F32)[:, None] * inv[None, :]
    cos, sin = jnp.cos(ang), jnp.sin(ang)
    return jnp.concatenate([cos, cos], axis=-1), jnp.concatenate([-sin, sin], axis=-1)


def _rope(x, cosf, sinf):
    return x * cosf + pltpu.roll(x, SWA_HD // 2, axis=1) * sinf


def _swa_prompt_kernel(q_ref, k_ref, v_ref, cos_ref, sin_ref, k_prev, v_prev, o_ref, kr_ref, vo_ref,
                       qs, p_num, p_den, p_mx, *, t):
    del k_prev, v_prev
    qb = LANES
    cosf, sinf = cos_ref[...], sin_ref[...]
    qs[...] = _rope(q_ref[...], cosf, sinf) * (SWA_HD ** -0.5)
    kr_ref[...] = _rope(k_ref[...], cosf, sinf)
    vo_ref[...] = v_ref[...]

    m_i = lax.broadcasted_iota(jnp.int32, (qb, 2 * qb), 0)
    n_i = lax.broadcasted_iota(jnp.int32, (qb, 2 * qb), 1)
    mask_two = (n_i >= m_i) & (n_i <= m_i + qb)
    mask_one = (lax.broadcasted_iota(jnp.int32, (qb, qb), 1)
                <= lax.broadcasted_iota(jnp.int32, (qb, qb), 0))

    def block(br, q0, k0, nk, dil):
        if dil == 1:
            qi, ki = pl.ds(q0, qb), pl.ds(k0, nk)
        else:
            qi, ki = pl.ds(q0, qb, stride=dil), pl.ds(k0, nk, stride=dil)
        sc = _mm_nt(qs[qi, :], kr_ref[ki, :])
        sc = jnp.where(mask_one if nk == qb else mask_two, sc, NEG_BIG)
        mx = jnp.max(sc, axis=-1, keepdims=True)
        p = jnp.exp(sc - mx)
        p_num[br, qi, :] = _mm(p, v_ref[ki, :])
        p_den[br, qi, :] = jnp.broadcast_to(jnp.sum(p, axis=-1, keepdims=True), (qb, LANES))
        p_mx[br, qi, :] = jnp.broadcast_to(mx, (qb, LANES))

    for br, (window, dil) in enumerate(DILATIONS):
        assert window == qb * dil and t % (qb * dil) == 0
        nblk = t // (qb * dil)

        def residue(r, carry, br=br, dil=dil, nblk=nblk):
            block(br, r, r, qb, dil)
            if nblk > 1:
                def later(j, cc):
                    block(br, r + dil * qb * j, r + dil * qb * (j - 1), 2 * qb, dil)
                    return cc
                lax.fori_loop(1, nblk, later, 0, unroll=5 if (nblk - 1) % 5 == 0 else 3)
            return carry

        if dil == 1:
            residue(0, 0)
        else:
            lax.fori_loop(0, dil, residue, 0, unroll=8 if nblk == 1 else 2)

    rows = 2 * qb

    def merge(i, carry):
        sl = pl.ds(pl.multiple_of(i * rows, rows), rows)
        mxs = [p_mx[br, sl, :] for br in range(len(DILATIONS))]
        mx = functools.reduce(jnp.maximum, mxs)
        wts = [jnp.exp(m - mx) for m in mxs]
        num = sum(p_num[br, sl, :] * wts[br] for br in range(len(DILATIONS)))
        den = sum(p_den[br, sl, :] * wts[br] for br in range(len(DILATIONS)))
        o_ref[sl, :] = (num / den).astype(o_ref.dtype)
        return carry

    lax.fori_loop(0, t // rows, merge, 0)


def _swa_prompt(proj, layer, k_all, v_all, b, t):
    cosf, sinf = _rope_tables(0, t)
    hq, hk, hv = PK_AQ // LANES, PK_AK // LANES, PK_AV // LANES
    kv_spec = pl.BlockSpec((None, None, None, t, SWA_HD), lambda bi, h: (layer, bi, h, 0, 0))
    hbm = pl.BlockSpec(memory_space=pl.ANY)
    return pl.pallas_call(
        functools.partial(_swa_prompt_kernel, t=t),
        grid=(b, SWA_HEADS),
        in_specs=[pl.BlockSpec((t, LANES), lambda bi, h: (bi, hq + h)),
                  pl.BlockSpec((t, LANES), lambda bi, h: (bi, hk + h)),
                  pl.BlockSpec((t, LANES), lambda bi, h: (bi, hv + h)),
                  pl.BlockSpec((t, LANES), lambda bi, h: (0, 0)),
                  pl.BlockSpec((t, LANES), lambda bi, h: (0, 0)),
                  hbm, hbm],
        out_specs=[pl.BlockSpec((t, LANES), lambda bi, h: (bi, h)), kv_spec, kv_spec],
        out_shape=[jax.ShapeDtypeStruct((b * t, SWA_W), BF16),
                   jax.ShapeDtypeStruct(k_all.shape, F32), jax.ShapeDtypeStruct(v_all.shape, F32)],
        input_output_aliases={5: 1, 6: 2},
        scratch_shapes=[pltpu.VMEM((t, LANES), F32)] + [pltpu.VMEM((len(DILATIONS), t, LANES), F32)] * 3,
        compiler_params=_cparams(("arbitrary", "arbitrary")),
        name="swa_prompt",
    )(proj, proj, proj, cosf, sinf, k_all, v_all)


def _swa_sample_kernel(q_ref, k_ref, v_ref, cos_ref, sin_ref, kf, vf, kn, vn, o_ref, kr_ref, *, t, past):
    (w1, d1), (w2, d2), (w3, d3) = DILATIONS
    n3 = kf.shape[1] * kf.shape[2]
    n_l = kn.shape[1]
    assert d1 == 1 and t <= d3 and w3 == past and n_l == w2 and w1 <= n_l
    assert t & (t - 1) == 0 and d2 & (d2 - 1) == 0
    cosf, sinf = cos_ref[...], sin_ref[...]

    m3 = lax.broadcasted_iota(jnp.int32, (t, n3), 0)
    c3 = lax.broadcasted_iota(jnp.int32, (t, n3), 1)
    mask3 = (c3 & (t - 1)) == m3
    ml = lax.broadcasted_iota(jnp.int32, (t, n_l), 0)
    cl = lax.broadcasted_iota(jnp.int32, (t, n_l), 1)
    dist_l = n_l + ml - cl
    mask1l = dist_l <= w1
    mask2l = (dist_l <= w2) & ((dist_l & (d2 - 1)) == 0)
    mn = lax.broadcasted_iota(jnp.int32, (t, t), 0)
    cn = lax.broadcasted_iota(jnp.int32, (t, t), 1)
    dist_n = mn - cn
    mask1n = dist_n >= 0
    mask2n = (dist_n >= 0) & ((dist_n & (d2 - 1)) == 0)
    mask3n = dist_n == 0

    for h in range(SWA_HEADS):
        hs = slice(h * SWA_HD, (h + 1) * SWA_HD)
        qh = _rope(q_ref[:, hs], cosf, sinf) * (SWA_HD ** -0.5)
        k_new = _rope(k_ref[:, hs], cosf, sinf)
        kr_ref[:, hs] = k_new
        v_new = v_ref[:, hs]
        k3 = kf[h].reshape(n3, SWA_HD)
        v3 = vf[h].reshape(n3, SWA_HD)
        kl = kn[h]
        vl = vn[h]
        s3 = _mm_nt(qh, k3)
        s_l = _mm_nt(qh, kl)
        s_n = _mm_nt(qh, k_new)

        def branch(parts):
            mx = None
            for sc, mask in parts:
                cur = jnp.max(jnp.where(mask, sc, NEG_BIG), axis=-1, keepdims=True)
                mx = cur if mx is None else jnp.maximum(mx, cur)
            ps = [jnp.where(mask, jnp.exp(jnp.where(mask, sc, NEG_BIG) - mx), 0.0) for sc, mask in parts]
            den = sum(jnp.sum(p, axis=-1, keepdims=True) for p in ps)
            return mx, ps, den

        mx1, (p1l, p1n), den1 = branch([(s_l, mask1l), (s_n, mask1n)])
        mx2, (p2l, p2n), den2 = branch([(s_l, mask2l), (s_n, mask2n)])
        mx3, (p3c, p3n), den3 = branch([(s3, mask3), (s_n, mask3n)])
        mx = jnp.maximum(jnp.maximum(mx1, mx2), mx3)
        wt1, wt2, wt3 = jnp.exp(mx1 - mx), jnp.exp(mx2 - mx), jnp.exp(mx3 - mx)
        num = (_mm(p1l * wt1 + p2l * wt2, vl) + _mm(p3c * wt3, v3)
               + _mm(p1n * wt1 + p2n * wt2 + p3n * wt3, v_new))
        den = den1 * wt1 + den2 * wt2 + den3 * wt3
        o_ref[:, hs] = (num / den).astype(o_ref.dtype)


def _swa_sample(proj, cache_k, cache_v, layer, b, t, past):
    depth, _, l_cache, nh, hd = cache_k.shape
    assert l_cache == past and nh * hd == SWA_W
    (_, _), (w2, _), (_, d3) = DILATIONS
    cosf, sinf = _rope_tables(past, t)
    assert l_cache % d3 == 0 and l_cache % w2 == 0 and t % SUBLANES == 0
    ckh = jnp.swapaxes(cache_k, 2, 3)
    cvh = jnp.swapaxes(cache_v, 2, 3)
    far = pl.BlockSpec((None, None, nh, l_cache // d3, t, hd), lambda bi: (layer, bi, 0, 0, 0, 0))
    near = pl.BlockSpec((None, None, nh, w2, hd), lambda bi: (layer, bi, 0, l_cache // w2 - 1, 0))
    split = lambda c: c.reshape(depth, b, nh, l_cache // d3, d3, hd)
    return pl.pallas_call(
        functools.partial(_swa_sample_kernel, t=t, past=past),
        grid=(b,),
        in_specs=[pl.BlockSpec((t, SWA_W), lambda bi: (bi, PK_AQ // SWA_W)),
                  pl.BlockSpec((t, SWA_W), lambda bi: (bi, PK_AK // SWA_W)),
                  pl.BlockSpec((t, SWA_W), lambda bi: (bi, PK_AV // SWA_W)),
                  pl.BlockSpec((t, LANES), lambda bi: (0, 0)),
                  pl.BlockSpec((t, LANES), lambda bi: (0, 0)),
                  far, far, near, near],
        out_specs=[pl.BlockSpec((t, SWA_W), lambda bi: (bi, 0)),
                   pl.BlockSpec((t, SWA_W), lambda bi: (bi, 0))],
        out_shape=[jax.ShapeDtypeStruct((b * t, SWA_W), BF16),
                   jax.ShapeDtypeStruct((b * t, SWA_W), F32)],
        compiler_params=_cparams(("arbitrary",)),
        name="swa_sample",
    )(proj, proj, proj, cosf, sinf, split(ckh), split(cvh), ckh, cvh)


def _outproj_kernel(oa_ref, ob_ref, oc_ref, w_ref, x_ref, gate_ref, gpost_ref, gpre_ref, sh_ref, sc_ref,
                    xo_ref, h2_ref):
    mix = (jnp.dot(oa_ref[...], w_ref[0:GDN_W, :], preferred_element_type=F32)
           + jnp.dot(ob_ref[...], w_ref[GDN_W:GDN_W + SSM_W, :], preferred_element_type=F32)
           + jnp.dot(oc_ref[...], w_ref[GDN_W + SSM_W:, :], preferred_element_type=F32))
    y = _rms(mix, gpost_ref[...])
    x = x_ref[...] + gate_ref[...] * y.reshape(x_ref.shape)
    xo_ref[...] = x
    h2 = _rms(x, gpre_ref[...]) * (1.0 + sc_ref[...]) + sh_ref[...]
    h2_ref[...] = h2.reshape(h2_ref.shape).astype(h2_ref.dtype)


def _outproj(oa, ob, oc, w, x, mod, gpost, gpre, layer, bb, tt):
    b, t, d = x.shape
    nt = t // tt
    rows = bb * tt
    rmap = lambda bi, ti: (bi * nt + ti, 0)
    mspec = lambda k: pl.BlockSpec((None, bb, 1, d), lambda bi, ti: (layer, bi, 0, k))
    gspec = pl.BlockSpec((None, 1, d), lambda bi, ti: (layer, 0, 0))
    return pl.pallas_call(
        _outproj_kernel,
        grid=(b // bb, nt),
        in_specs=[pl.BlockSpec((rows, GDN_W), rmap), pl.BlockSpec((rows, SSM_W), rmap),
                  pl.BlockSpec((rows, SWA_W), rmap),
                  pl.BlockSpec((None, d, d), lambda bi, ti: (layer, 0, 0)),
                  pl.BlockSpec((bb, tt, d), lambda bi, ti: (bi, ti, 0)),
                  mspec(2), gspec, gspec, mspec(3), mspec(4)],
        out_specs=[pl.BlockSpec((bb, tt, d), lambda bi, ti: (bi, ti, 0)),
                   pl.BlockSpec((rows, d), rmap)],
        out_shape=[jax.ShapeDtypeStruct((b, t, d), F32), jax.ShapeDtypeStruct((b * t, d), BF16)],
        compiler_params=_cparams(("arbitrary", "arbitrary")),
        name="outproj",
    )(oa, ob, oc, w, x, mod, gpost, gpre, mod, mod)


def _ffn_up_kernel(*refs, tt, zero_init):
    if zero_init:
        (h_ref, wg_ref, wu_ref, cwg_ref, cwu_ref, cbg_ref, cbu_ref, act_ref, nsg_ref, nsu_ref, csg, csu) = refs
        stg_ref = stu_ref = None
    else:
        (h_ref, wg_ref, wu_ref, cwg_ref, cwu_ref, cbg_ref, cbu_ref, stg_ref, stu_ref,
         act_ref, nsg_ref, nsu_ref, csg, csu) = refs
    first = pl.program_id(2) == 0
    lo = SUBLANES - (FFN_CONV - 1)
    h = h_ref[...]

    def half(w_ref, cw_ref, cb_ref, st_ref, ns_ref, cs):
        bb = cs.shape[0]

        @pl.when(first)
        def _():
            if st_ref is None:
                cs[:, lo:SUBLANES, :] = jnp.zeros((bb, FFN_CONV - 1, cs.shape[2]), F32)
            else:
                cs[:, lo:SUBLANES, :] = st_ref[...]

        @pl.when(jnp.logical_not(first))
        def _():
            cs[:, lo:SUBLANES, :] = cs[:, tt + lo:tt + SUBLANES, :]

        up = jnp.dot(h, w_ref[...], preferred_element_type=F32)
        cs[:, SUBLANES:SUBLANES + tt, :] = up.reshape(bb, tt, up.shape[-1])
        y = cs[:, lo:lo + tt, :] * cw_ref[0:1, :]
        for j in range(1, FFN_CONV):
            y = y + cs[:, lo + j:lo + j + tt, :] * cw_ref[j:j + 1, :]
        ns_ref[...] = cs[:, tt + lo:tt + SUBLANES, :]
        return y + cb_ref[...]

    yg = half(wg_ref, cwg_ref, cbg_ref, stg_ref, nsg_ref, csg)
    yu = half(wu_ref, cwu_ref, cbu_ref, stu_ref, nsu_ref, csu)
    act = _silu(yg) * yu
    act_ref[...] = act.reshape(act_ref.shape).astype(act_ref.dtype)


def _ffn_up(h2, state, w_up, conv_w, conv_b, layer, b, t, bb, tt):
    zero_init = state is None
    d = h2.shape[1]
    tn = 512
    nh = D_FF // tn
    nt = t // tt
    rows = bb * tt
    depth = conv_w.shape[0]
    cb = conv_b.reshape(depth, 1, -1)
    in_specs = [pl.BlockSpec((rows, d), lambda bi, j, ti: (bi * nt + ti, 0)),
                pl.BlockSpec((None, d, tn), lambda bi, j, ti: (layer, 0, j)),
                pl.BlockSpec((None, d, tn), lambda bi, j, ti: (layer, 0, nh + j)),
                pl.BlockSpec((None, FFN_CONV, tn), lambda bi, j, ti: (layer, 0, j)),
                pl.BlockSpec((None, FFN_CONV, tn), lambda bi, j, ti: (layer, 0, nh + j)),
                pl.BlockSpec((None, 1, tn), lambda bi, j, ti: (layer, 0, j)),
                pl.BlockSpec((None, 1, tn), lambda bi, j, ti: (layer, 0, nh + j))]
    args = [h2, w_up, w_up, conv_w, conv_w, cb, cb]
    if not zero_init:
        in_specs += [pl.BlockSpec((None, bb, FFN_CONV - 1, tn), lambda bi, j, ti: (layer, bi, 0, j)),
                     pl.BlockSpec((None, bb, FFN_CONV - 1, tn), lambda bi, j, ti: (layer, bi, 0, nh + j))]
        args += [state, state]
    ns_spec = pl.BlockSpec((bb, FFN_CONV - 1, tn), lambda bi, j, ti: (bi, 0, j))
    act, nsg, nsu = pl.pallas_call(
        functools.partial(_ffn_up_kernel, tt=tt, zero_init=zero_init),
        grid=(b // bb, nh, nt),
        in_specs=in_specs,
        out_specs=[pl.BlockSpec((rows, tn), lambda bi, j, ti: (bi * nt + ti, j)), ns_spec, ns_spec],
        out_shape=[jax.ShapeDtypeStruct((b * t, D_FF), BF16),
                   jax.ShapeDtypeStruct((b, FFN_CONV - 1, D_FF), F32),
                   jax.ShapeDtypeStruct((b, FFN_CONV - 1, D_FF), F32)],
        scratch_shapes=[pltpu.VMEM((bb, tt + SUBLANES, tn), F32)] * 2,
        compiler_params=_cparams(("arbitrary", "arbitrary", "arbitrary")),
        name="ffn_up",
    )(*args)
    return act, jnp.concatenate([nsg, nsu], axis=-1)


def _ffn_down_kernel(a_ref, w_ref, x_ref, gate_ref, g_ref, o_ref):
    k = pl.program_id(2)
    part = lambda: jnp.dot(a_ref[...], w_ref[...], preferred_element_type=F32).reshape(o_ref.shape)

    @pl.when(k == 0)
    def _():
        o_ref[...] = part()

    @pl.when(k > 0)
    def _():
        o_ref[...] += part()

    @pl.when(k == pl.num_programs(2) - 1)
    def _():
        o_ref[...] = x_ref[...] + gate_ref[...] * _rms(o_ref[...], g_ref[...])


def _ffn_down(act, w_down, x, mod, g, layer, bb, tt):
    b, t, d = x.shape
    nt = t // tt
    rows = bb * tt
    tk = 512
    return pl.pallas_call(
        _ffn_down_kernel,
        grid=(b // bb, nt, D_FF // tk),
        in_specs=[pl.BlockSpec((rows, tk), lambda bi, ti, k: (bi * nt + ti, k)),
                  pl.BlockSpec((None, tk, d), lambda bi, ti, k: (layer, k, 0)),
                  pl.BlockSpec((bb, tt, d), lambda bi, ti, k: (bi, ti, 0)),
                  pl.BlockSpec((None, bb, 1, d), lambda bi, ti, k: (layer, bi, 0, 5)),
                  pl.BlockSpec((None, 1, d), lambda bi, ti, k: (layer, 0, 0))],
        out_specs=pl.BlockSpec((bb, tt, d), lambda bi, ti, k: (bi, ti, 0)),
        out_shape=jax.ShapeDtypeStruct((b, t, d), F32),
        compiler_params=_cparams(("arbitrary", "arbitrary", "arbitrary")),
        name="ffn_down",
    )(act, w_down, x, mod, g)


def _pack_w_in(w_in):
    depth, d, _ = w_in.shape
    o_gb = 4 * GDN_W
    o_sz = o_gb + 2 * GDN_HEADS
    o_dt = o_sz + SSM_W + SSM_CONV_CH
    o_aq = o_dt + SSM_HEADS
    pad = jnp.zeros((depth, d, LANES - 2 * GDN_HEADS - SSM_HEADS), w_in.dtype)
    packed = jnp.concatenate([w_in[..., :o_gb], w_in[..., o_sz:o_dt], w_in[..., o_aq:],
                              w_in[..., o_gb:o_sz], w_in[..., o_dt:o_aq], pad], axis=-1)
    assert packed.shape[-1] == PK_COLS
    return packed.astype(BF16)


def _layer(x, mod, layer, states, kv_all, wts, past, bb, tt):
    b, t, d = x.shape
    if states is None:
        gdn_conv = gdn_s = ssm_conv = ssm_h = cache_k = cache_v = ffn_conv = None
    else:
        gdn_conv, gdn_s, ssm_conv, ssm_h, cache_k, cache_v, ffn_conv = states
    tt2 = 2 * tt if (bb == 1 and t % (2 * tt) == 0) else tt
    proj = _inproj(x, mod, wts["g_pre_mix"], wts["w_in"], layer, bb, tt2)
    gdn_w = (wts["gdn_conv_w"], wts["gdn_a_log"], wts["gdn_dt_bias"], wts["gdn_norm_g"])
    if states is None:
        o_a, new_gdn_s = _gdn_seq(proj, *gdn_w, layer, b, t)
    else:
        o_a, new_gdn_s = _gdn_step(proj, gdn_conv, gdn_s, *gdn_w, layer, b, t)
    o_b, new_ssm_h = _ssd(proj, ssm_conv, ssm_h, wts["ssm_conv_w"], wts["ssm_conv_b"], wts["ssm_dt_bias"],
                          wts["ssm_a_log"], wts["ssm_d"], wts["ssm_norm_g"], layer, b, t)
    if states is None:
        o_c, *kv_all = _swa_prompt(proj, layer, *kv_all, b, t)
    else:
        o_c, k_rot = _swa_sample(proj, cache_k, cache_v, layer, b, t, past)
    x_mid, h2 = _outproj(o_a, o_b, o_c, wts["w_out"], x, mod, wts["g_post_mix"], wts["g_pre_ffn"], layer, bb, tt)
    act, new_ffn_conv = _ffn_up(h2, ffn_conv, wts["w_up"], wts["ffn_conv_w"], wts["ffn_conv_b"], layer, b, t, bb, tt2)
    x_out = _ffn_down(act, wts["w_down"], x_mid, mod, wts["g_post_ffn"], layer, bb, tt2)

    proj3 = proj.reshape(b, t, PK_COLS)
    new_gdn_conv = proj3[:, t - (GDN_CONV - 1):, PK_GQ:PK_GQ + 3 * GDN_W]
    new_ssm_conv = proj3[:, t - (SSM_CONV - 1):, PK_SX:PK_SX + SSM_CONV_CH]
    outs = [new_gdn_conv, new_gdn_s, new_ssm_conv, new_ssm_h, new_ffn_conv]
    if states is not None:
        outs += [k_rot.reshape(b, t, SWA_HEADS, SWA_HD), proj3[:, :, PK_AV:PK_AV + SWA_W].reshape(b, t, SWA_HEADS, SWA_HD)]
    return x_out, outs, kv_all


def kernel(x_prompt, x_sample, c_prompt, c_sample, state_gdn_conv, state_gdn, state_ssm_conv, state_ssm, cache_k, cache_v, state_ffn_conv, w_ada, b_ada, g_pre_mix, g_post_mix, g_pre_ffn, g_post_ffn, w_in, gdn_conv_w, gdn_a_log, gdn_dt_bias, gdn_norm_g, ssm_conv_w, ssm_conv_b, ssm_dt_bias, ssm_a_log, ssm_d, ssm_norm_g, w_out, w_up, ffn_conv_w, ffn_conv_b, w_down):
    depth = w_ada.shape[0]
    bp, tp, d = x_prompt.shape
    bs, ts, _ = x_sample.shape
    past = cache_k.shape[2]
    assert tp >= GDN_CONV and ts >= GDN_CONV and ts % SUBLANES == 0

    vec = lambda a: a.reshape(depth, 1, a.shape[-1])
    wts = dict(
        g_pre_mix=vec(g_pre_mix), g_post_mix=vec(g_post_mix), g_pre_ffn=vec(g_pre_ffn), g_post_ffn=vec(g_post_ffn),
        w_in=_pack_w_in(w_in), w_out=w_out.astype(BF16), w_up=w_up.astype(BF16), w_down=w_down.astype(BF16),
        gdn_conv_w=gdn_conv_w, gdn_a_log=gdn_a_log, gdn_dt_bias=gdn_dt_bias, gdn_norm_g=gdn_norm_g,
        ssm_conv_w=ssm_conv_w, ssm_conv_b=ssm_conv_b, ssm_dt_bias=ssm_dt_bias, ssm_a_log=ssm_a_log, ssm_d=ssm_d,
        ssm_norm_g=ssm_norm_g, ffn_conv_w=ffn_conv_w, ffn_conv_b=ffn_conv_b)

    mod = _ada(jnp.concatenate([c_prompt, c_sample], axis=0), w_ada, b_ada)
    mod_p = mod[:, :bp].reshape(depth, bp, 1, 6 * d)
    mod_s = mod[:, bp:].reshape(depth, bs, 1, 6 * d)
    s_states = (state_gdn_conv, state_gdn, state_ssm_conv, state_ssm, cache_k, cache_v, state_ffn_conv)

    tt_p = math.gcd(tp, 512)
    bb_s = math.gcd(bs, 512 // ts)
    assert tp <= W_MAX and ts <= W_MAX
    y_p, y_s = x_prompt, x_sample
    kv_p = [jnp.zeros((depth, bp, SWA_HEADS, tp, SWA_HD), F32) for _ in range(2)]
    p_rows, s_rows = [], []
    for layer in range(depth):
        y_p, outs_p, kv_p = _layer(y_p, mod_p, layer, None, kv_p, wts, 0, 1, tt_p)
        y_s, outs_s, _ = _layer(y_s, mod_s, layer, s_states, None, wts, past, bb_s, ts)
        p_rows.append(outs_p)
        s_rows.append(outs_s)
    p_gdn_conv, p_gdn, p_ssm_conv, p_ssm, p_ffn_conv = [jnp.stack(a) for a in zip(*p_rows)]
    s_gdn_conv, s_gdn, s_ssm_conv, s_ssm, s_ffn_conv, s_k, s_v = [jnp.stack(a) for a in zip(*s_rows)]
    p_k, p_v = [jnp.swapaxes(a, 2, 3) for a in kv_p]
    return (y_p, y_s, p_gdn_conv, p_gdn, p_ssm_conv, p_ssm, p_k, p_v, p_ffn_conv,
            s_gdn_conv, s_gdn, s_ssm_conv, s_ssm, s_k, s_v, s_ffn_conv)
```

```python
import functools
import math

import jax
import jax.numpy as jnp
import numpy as np
from jax import lax
from jax.experimental import pallas as pl
from jax.experimental.pallas import tpu as pltpu

F32 = jnp.float32
BF16 = jnp.bfloat16

D_MODEL = 2048
MIX_UNIT = D_MODEL // 8
GDN_W = 3 * MIX_UNIT
SSM_W = 2 * MIX_UNIT
SWA_W = 3 * MIX_UNIT
GDN_DK = 128
GDN_DV = 128
GDN_HEADS = GDN_W // GDN_DV
GDN_CONV = 4
GDN_CHUNK = 64
SSM_HEADDIM = 64
SSM_HEADS = SSM_W // SSM_HEADDIM
SSM_GROUPS = 2
SSM_STATE = 128
SSM_CONV = 4
SSM_CHUNK = 64
SSM_CONV_CH = SSM_W + 2 * SSM_GROUPS * SSM_STATE
SWA_HD = 128
SWA_HEADS = SWA_W // SWA_HD
DILATIONS = ((128, 1), (512, 4), (2048, 16))
W_MAX = 2048
ROPE_THETA = 10000.0
D_FF = 11 * D_MODEL // 4
FFN_CONV = 3
NORM_EPS = 1e-6

LANES = 128
SUBLANES = 8
VMEM_LIMIT = 56 * 1024 * 1024

PK_GQ = 0
PK_GK = PK_GQ + GDN_W
PK_GV = PK_GK + GDN_W
PK_GZ = PK_GV + GDN_W
PK_SZ = PK_GZ + GDN_W
PK_SX = PK_SZ + SSM_W
PK_SBC = PK_SX + SSM_W
PK_AQ = PK_SX + SSM_CONV_CH
PK_AK = PK_AQ + SWA_W
PK_AV = PK_AK + SWA_W
PK_SM = PK_AV + SWA_W
PK_COLS = PK_SM + LANES
SM_BETA = 0
SM_A = GDN_HEADS
SM_DT = 2 * GDN_HEADS
NEG_BIG = -1e30
GDN_SEQ_HEADS = 3
GDN_SEQ_CHUNKS = 8


def _cparams(sem):
    return pltpu.CompilerParams(dimension_semantics=sem, vmem_limit_bytes=VMEM_LIMIT)


def _sigmoid(x):
    return 1.0 / (1.0 + jnp.exp(-x))


def _silu(x):
    return x * _sigmoid(x)


def _softplus(x):
    return jnp.maximum(x, 0.0) + jnp.log1p(jnp.exp(-jnp.abs(x)))


def _mm(a, b):
    return jnp.dot(a.astype(BF16), b.astype(BF16), preferred_element_type=F32)


def _mm_nt(a, b):
    return lax.dot_general(a.astype(BF16), b.astype(BF16), (((1,), (1,)), ((), ())), preferred_element_type=F32)


def _mm_hi(a, b):
    return jnp.dot(a, b, preferred_element_type=F32, precision=lax.Precision.HIGHEST)


def _rms(x, g):
    return x * lax.rsqrt(jnp.mean(x * x, axis=-1, keepdims=True) + NORM_EPS) * g


def _ada_kernel(c_ref, w_ref, b_ref, o_ref):
    a = _silu(c_ref[...]).astype(BF16)
    o_ref[...] = jnp.dot(a, w_ref[...].astype(BF16), preferred_element_type=F32) + b_ref[...]


def _ada(c_all, w_ada, b_ada):
    depth, d, n = w_ada.shape
    r = c_all.shape[0]
    tn = 1024
    return pl.pallas_call(
        _ada_kernel,
        grid=(depth, n // tn),
        in_specs=[pl.BlockSpec((r, d), lambda l, j: (0, 0)),
                  pl.BlockSpec((None, d, tn), lambda l, j: (l, 0, j)),
                  pl.BlockSpec((None, 1, tn), lambda l, j: (l, 0, j))],
        out_specs=pl.BlockSpec((None, r, tn), lambda l, j: (l, 0, j)),
        out_shape=jax.ShapeDtypeStruct((depth, r, n), F32),
        compiler_params=_cparams(("arbitrary", "arbitrary")),
        name="ada",
    )(c_all, w_ada, b_ada.reshape(depth, 1, n))


def _inproj_kernel(x_ref, sh_ref, sc_ref, g_ref, w_ref, o_ref, h_scr):
    @pl.when(pl.program_id(2) == 0)
    def _():
        h = _rms(x_ref[...], g_ref[...]) * (1.0 + sc_ref[...]) + sh_ref[...]
        h_scr[...] = h.reshape(h_scr.shape).astype(BF16)

    o_ref[...] = jnp.dot(h_scr[...], w_ref[...], preferred_element_type=F32)


def _inproj(x, mod, g, w, layer, bb, tt):
    b, t, d = x.shape
    n = w.shape[-1]
    tn = 1408
    nt = t // tt
    return pl.pallas_call(
        _inproj_kernel,
        grid=(b // bb, nt, n // tn),
        in_specs=[pl.BlockSpec((bb, tt, d), lambda bi, ti, j: (bi, ti, 0)),
                  pl.BlockSpec((None, bb, 1, d), lambda bi, ti, j: (layer, bi, 0, 0)),
                  pl.BlockSpec((None, bb, 1, d), lambda bi, ti, j: (layer, bi, 0, 1)),
                  pl.BlockSpec((None, 1, d), lambda bi, ti, j: (layer, 0, 0)),
                  pl.BlockSpec((None, d, tn), lambda bi, ti, j: (layer, 0, j))],
        out_specs=pl.BlockSpec((bb * tt, tn), lambda bi, ti, j: (bi * nt + ti, j)),
        out_shape=jax.ShapeDtypeStruct((b * t, n), F32),
        scratch_shapes=[pltpu.VMEM((bb * tt, d), BF16)],
        compiler_params=_cparams(("arbitrary", "arbitrary", "arbitrary")),
        name="inproj",
    )(x, mod, mod, g, w)


def _conv_from_scratch(xs, w_ref, width, t):
    lo = SUBLANES - (width - 1)
    y = xs[lo:lo + t, :] * w_ref[0:1, :]
    for j in range(1, width):
        y = y + xs[lo + j:lo + j + t, :] * w_ref[j:j + 1, :]
    return y


def _conv_stage(xs, x_new, hist_ref, first, width, t):
    lo = SUBLANES - (width - 1)

    @pl.when(first)
    def _():
        if hist_ref is None:
            xs[lo:SUBLANES, :] = jnp.zeros((width - 1, xs.shape[1]), F32)
        else:
            xs[lo:SUBLANES, :] = hist_ref[...]

    @pl.when(jnp.logical_not(first))
    def _():
        xs[lo:SUBLANES, :] = xs[t + lo:t + SUBLANES, :]

    xs[SUBLANES:SUBLANES + t, :] = x_new


def _tri_masks(c):
    row = lax.broadcasted_iota(jnp.int32, (c, c), 0)
    col = lax.broadcasted_iota(jnp.int32, (c, c), 1)
    return row == col, row >= col, row > col, row <= col


def _bdot(a, b, ca, cb):
    return lax.dot_general(a, b, (((ca,), (cb,)), ((0,), (0,))), preferred_element_type=F32)


def _bmm(a, b):
    return _bdot(a.astype(BF16), b.astype(BF16), 2, 1)


def _bmm_nt(a, b):
    return _bdot(a.astype(BF16), b.astype(BF16), 2, 2)


def _bmm_tn(a, b):
    return _bdot(a.astype(BF16), b.astype(BF16), 1, 1)


def _split_bf16(a):
    hi = a.astype(BF16)
    return hi, (a - hi.astype(F32)).astype(BF16)


def _bmm3(a, b):
    ah, al = _split_bf16(a)
    bh, bl = _split_bf16(b)
    return _bdot(ah, bh, 2, 1) + _bdot(ah, bl, 2, 1) + _bdot(al, bh, 2, 1)


def _unit_lower_inverse(a, eye_f, c):
    x = eye_f - a
    p = _bmm3(a, a)
    n = 2
    while True:
        x = x + _bmm3(x, p)
        n *= 2
        if n >= c:
            return x
        p = _bmm3(p, p)


def _gdn_prep(q, k, v, beta, g, c):
    eye, causal, strict, upper = _tri_masks(c)
    g_row = jnp.sum(jnp.where(eye, g, 0.0), axis=1, keepdims=True)
    cg_col = jnp.sum(jnp.where(causal, g_row, 0.0), axis=2, keepdims=True)
    cg_row = jnp.sum(jnp.where(upper, g, 0.0), axis=1, keepdims=True)
    gam = jnp.where(causal, jnp.exp(jnp.where(causal, cg_col - cg_row, 0.0)), 0.0)
    kb = k * beta
    amat = jnp.where(strict, _bmm_nt(kb, k) * gam, 0.0)
    tinv = _unit_lower_inverse(amat, eye.astype(F32), c)
    ecg = jnp.exp(cg_col)
    u = _bmm3(tinv, v * beta)
    w = _bmm3(tinv, kb * ecg)
    qk = _bmm_nt(q, k) * gam
    cg_last = cg_col[:, c - 1:c, :]
    kd = k * jnp.exp(cg_last - cg_col)
    return u, w, qk, q * ecg, kd, jnp.exp(cg_last)


def _gdn_gates(b_raw, a_raw, a_log, dt_bias):
    a_neg = -jnp.exp(jnp.full((1, 1), a_log, F32))
    return _sigmoid(b_raw), a_neg * _softplus(a_raw + dt_bias)


def _l2norm(x):
    return x * lax.rsqrt(jnp.sum(x * x, axis=-1, keepdims=True) + NORM_EPS)


def _gdn_seq_kernel(alog_ref, dtb_ref, q_ref, k_ref, v_ref, z_ref, sm_ref, wq_ref, wk_ref, wv_ref, ng_ref,
                    o_ref, sfin_ref, xq, xk, xv, s_scr, *, c, nc, hp):
    hg = pl.program_id(1)
    tb = c * nc
    first = pl.program_id(2) == 0

    @pl.when(first)
    def _():
        s_scr[...] = jnp.zeros_like(s_scr)

    _conv_stage(xq, q_ref[...], None, first, GDN_CONV, tb)
    _conv_stage(xk, k_ref[...], None, first, GDN_CONV, tb)
    _conv_stage(xv, v_ref[...], None, first, GDN_CONV, tb)
    qa = _silu(_conv_from_scratch(xq, wq_ref, GDN_CONV, tb))
    ka = _silu(_conv_from_scratch(xk, wk_ref, GDN_CONV, tb))
    va = _silu(_conv_from_scratch(xv, wv_ref, GDN_CONV, tb))

    sm = sm_ref[...]
    lane = lax.broadcasted_iota(jnp.int32, sm.shape, 1)
    chunks = lambda a: a.reshape(nc, c, a.shape[-1])
    parts = []
    for j in range(hp):
        h = hg * hp + j
        hs = slice(j * GDN_DK, (j + 1) * GDN_DK)
        b_raw = jnp.sum(jnp.where(lane == SM_BETA + h, sm, 0.0), axis=1, keepdims=True)
        a_raw = jnp.sum(jnp.where(lane == SM_A + h, sm, 0.0), axis=1, keepdims=True)
        beta, g = _gdn_gates(b_raw, a_raw, alog_ref[h], dtb_ref[h])
        parts.append([chunks(a) for a in (_l2norm(qa[:, hs]) * (GDN_DK ** -0.5), _l2norm(ka[:, hs]), va[:, hs],
                                          beta, g)])
    u, w, qk, qg, kd, gl = _gdn_prep(*[jnp.concatenate(a, axis=0) for a in zip(*parts)], c)

    ng = ng_ref[...]
    states = [s_scr[j] for j in range(hp)]
    for ci in range(nc):
        sl = slice(ci * c, (ci + 1) * c)
        for j in range(hp):
            p = j * nc + ci
            hs = slice(j * GDN_DV, (j + 1) * GDN_DV)
            ws = _mm(jnp.concatenate([w[p], qg[p]], axis=0), states[j])
            vnew = u[p] - ws[:c]
            r = _mm(jnp.concatenate([qk[p], kd[p].T], axis=0), vnew)
            states[j] = states[j] * gl[p] + r[c:]
            o = _rms(ws[c:] + r[:c], ng) * _silu(z_ref[sl, hs])
            o_ref[sl, hs] = o.astype(o_ref.dtype)
    for j in range(hp):
        s_scr[j] = states[j]
        sfin_ref[j] = states[j]


def _gdn_seq(proj, conv_w, a_log, dt_bias, norm_g, layer, b, t):
    c = math.gcd(t, GDN_CHUNK)
    nc = math.gcd(t // c, GDN_SEQ_CHUNKS)
    hp = GDN_SEQ_HEADS
    assert GDN_HEADS % hp == 0
    tb = c * nc
    nt = t // tb
    wide = hp * LANES
    hq, hk, hv, hz = PK_GQ // wide, PK_GK // wide, PK_GV // wide, PK_GZ // wide
    smem = pl.BlockSpec(memory_space=pltpu.SMEM)
    cblk = lambda off: pl.BlockSpec((tb, wide), lambda bi, h, i: (bi * nt + i, off + h))
    wblk = lambda off: pl.BlockSpec((None, GDN_CONV, wide), lambda bi, h, i: (layer, 0, off + h))
    return pl.pallas_call(
        functools.partial(_gdn_seq_kernel, c=c, nc=nc, hp=hp),
        grid=(b, GDN_HEADS // hp, nt),
        in_specs=[smem, smem, cblk(hq), cblk(hk), cblk(hv), cblk(hz),
                  pl.BlockSpec((tb, LANES), lambda bi, h, i: (bi * nt + i, PK_SM // LANES)),
                  wblk(0), wblk(GDN_HEADS // hp), wblk(2 * GDN_HEADS // hp),
                  pl.BlockSpec((None, 1, LANES), lambda bi, h, i: (layer, 0, 0))],
        out_specs=[cblk(0), pl.BlockSpec((None, hp, GDN_DK, GDN_DV), lambda bi, h, i: (bi, h, 0, 0))],
        out_shape=[jax.ShapeDtypeStruct((b * t, GDN_W), BF16),
                   jax.ShapeDtypeStruct((b, GDN_HEADS, GDN_DK, GDN_DV), F32)],
        scratch_shapes=[pltpu.VMEM((tb + SUBLANES, wide), F32)] * 3 + [pltpu.VMEM((hp, GDN_DK, GDN_DV), F32)],
        compiler_params=_cparams(("arbitrary", "arbitrary", "arbitrary")),
        name="gdn_seq",
    )(a_log[layer], dt_bias[layer], proj, proj, proj, proj, proj, conv_w, conv_w, conv_w,
      norm_g.reshape(norm_g.shape[0], 1, LANES))


def _gdn_step_kernel(alog_ref, dtb_ref, x_ref, z_ref, sm_ref, w_ref, ng_ref, hist_ref, s0_ref,
                     o_ref, sfin_ref, cs, *, bb, t):
    lo = SUBLANES - (GDN_CONV - 1)
    cs[:, lo:SUBLANES, :] = hist_ref[...]
    cs[:, SUBLANES:SUBLANES + t, :] = x_ref[...].reshape(bb, t, 3 * GDN_W)
    y = cs[:, lo:lo + t, :] * w_ref[0:1, :]
    for j in range(1, GDN_CONV):
        y = y + cs[:, lo + j:lo + j + t, :] * w_ref[j:j + 1, :]
    y = _silu(y)
    sm = sm_ref[...].reshape(bb, t, LANES)
    ng = ng_ref[...]
    for h in range(GDN_HEADS):
        hs = slice(h * GDN_DK, (h + 1) * GDN_DK)
        q = _l2norm(y[:, :, hs]) * (GDN_DK ** -0.5)
        k = _l2norm(y[:, :, GDN_W + h * GDN_DK:GDN_W + (h + 1) * GDN_DK])
        v = y[:, :, 2 * GDN_W + h * GDN_DV:2 * GDN_W + (h + 1) * GDN_DV]
        beta, g = _gdn_gates(sm[:, :, SM_BETA + h:SM_BETA + h + 1], sm[:, :, SM_A + h:SM_A + h + 1],
                             alog_ref[h], dtb_ref[h])
        u, w, qk, qg, kd, gl = _gdn_prep(q, k, v, beta, g, t)
        s = s0_ref[:, h]
        vnew = u - _bmm(w, s)
        o = _bmm(qg, s) + _bmm(qk, vnew)
        sfin_ref[:, h] = s * gl + _bmm_tn(kd, vnew)
        o = _rms(o, ng) * _silu(z_ref[:, hs].reshape(bb, t, GDN_DV))
        o_ref[:, hs] = o.reshape(bb * t, GDN_DV).astype(o_ref.dtype)


def _gdn_step(proj, conv_buf, s0, conv_w, a_log, dt_bias, norm_g, layer, b, t):
    assert t % SUBLANES == 0 and GDN_CHUNK % t == 0 and t >= GDN_CONV - 1
    bb = math.gcd(b, 8)
    rows = bb * t
    smem = pl.BlockSpec(memory_space=pltpu.SMEM)
    return pl.pallas_call(
        functools.partial(_gdn_step_kernel, bb=bb, t=t),
        grid=(b // bb,),
        in_specs=[smem, smem,
                  pl.BlockSpec((rows, 3 * GDN_W), lambda bi: (bi, PK_GQ // (3 * GDN_W))),
                  pl.BlockSpec((rows, GDN_W), lambda bi: (bi, PK_GZ // GDN_W)),
                  pl.BlockSpec((rows, LANES), lambda bi: (bi, PK_SM // LANES)),
                  pl.BlockSpec((None, GDN_CONV, 3 * GDN_W), lambda bi: (layer, 0, 0)),
                  pl.BlockSpec((None, 1, LANES), lambda bi: (layer, 0, 0)),
                  pl.BlockSpec((None, bb, GDN_CONV - 1, 3 * GDN_W), lambda bi: (layer, bi, 0, 0)),
                  pl.BlockSpec((None, bb, GDN_HEADS, GDN_DK, GDN_DV), lambda bi: (layer, bi, 0, 0, 0))],
        out_specs=[pl.BlockSpec((rows, GDN_W), lambda bi: (bi, 0)),
                   pl.BlockSpec((bb, GDN_HEADS, GDN_DK, GDN_DV), lambda bi: (bi, 0, 0, 0))],
        out_shape=[jax.ShapeDtypeStruct((b * t, GDN_W), BF16),
                   jax.ShapeDtypeStruct((b, GDN_HEADS, GDN_DK, GDN_DV), F32)],
        scratch_shapes=[pltpu.VMEM((bb, t + SUBLANES, 3 * GDN_W), F32)],
        compiler_params=_cparams(("arbitrary",)),
        name="gdn_step",
    )(a_log[layer], dt_bias[layer], proj, proj, proj, conv_w, norm_g.reshape(norm_g.shape[0], 1, LANES),
      conv_buf, s0)


def _ssd_kernel(*refs, c, nc, bb, zero_init):
    if zero_init:
        (z_ref, x_ref, bc_ref, sm_ref, wx_ref, wbc_ref, bx_ref, bbc_ref, dtb_ref, alog_ref, dsk_ref, ng_ref,
         y_ref, hfin_ref, xs, xbc, h_scr) = refs
        cx_ref = cbc_ref = h0_ref = None
    else:
        (z_ref, x_ref, bc_ref, sm_ref, wx_ref, wbc_ref, bx_ref, bbc_ref, dtb_ref, alog_ref, dsk_ref, ng_ref,
         cx_ref, cbc_ref, h0_ref, y_ref, hfin_ref, xs, xbc, h_scr) = refs
    tb = c * nc
    first = pl.program_id(1) == 0
    lo = SUBLANES - (SSM_CONV - 1)

    @pl.when(first)
    def _():
        if zero_init:
            h_scr[...] = jnp.zeros_like(h_scr)
        else:
            h_scr[...] = h0_ref[...]

    def conv(cs, new_ref, hist_ref, w_ref, b_ref):
        width = cs.shape[2]

        @pl.when(first)
        def _():
            if hist_ref is None:
                cs[:, lo:SUBLANES, :] = jnp.zeros((bb, SSM_CONV - 1, width), F32)
            else:
                cs[:, lo:SUBLANES, :] = hist_ref[...]

        @pl.when(jnp.logical_not(first))
        def _():
            cs[:, lo:SUBLANES, :] = cs[:, tb + lo:tb + SUBLANES, :]

        cs[:, SUBLANES:SUBLANES + tb, :] = new_ref[...].reshape(bb, tb, width)
        y = cs[:, lo:lo + tb, :] * w_ref[0:1, :]
        for j in range(1, SSM_CONV):
            y = y + cs[:, lo + j:lo + j + tb, :] * w_ref[j:j + 1, :]
        return _silu(y + b_ref[...])

    xv = conv(xs, x_ref, cx_ref, wx_ref, bx_ref)
    bcv = conv(xbc, bc_ref, cbc_ref, wbc_ref, bbc_ref)
    gs = SSM_GROUPS * SSM_STATE

    dt_all = _softplus(sm_ref[...] + dtb_ref[...]).reshape(bb, tb, LANES)
    da_all = dt_all * (-jnp.exp(alog_ref[...]))
    _, causal, _, _ = _tri_masks(c)
    tril_f = causal.astype(F32)
    rep = SSM_HEADS // SSM_GROUPS
    gw = SSM_W // SSM_GROUPS
    ng = ng_ref[...]
    for bi in range(bb):
        for ci in range(nc):
            sl = slice(ci * c, (ci + 1) * c)
            rows = slice(bi * tb + ci * c, bi * tb + (ci + 1) * c)
            cg_all = _mm_hi(tril_f, da_all[bi, sl])
            cg_t = cg_all.T
            ys = []
            for grp in range(SSM_GROUPS):
                bm = bcv[bi, sl, grp * SSM_STATE:(grp + 1) * SSM_STATE]
                cm = bcv[bi, sl, gs + grp * SSM_STATE:gs + (grp + 1) * SSM_STATE]
                cb = _mm_nt(cm, bm)
                for hh in range(rep):
                    hd = grp * rep + hh
                    ln = SM_DT + hd
                    cg_col = cg_all[:, ln:ln + 1]
                    cg_row = cg_t[ln:ln + 1, :]
                    lmat = jnp.where(causal, jnp.exp(jnp.where(causal, cg_col - cg_row, 0.0)), 0.0)
                    xh = xv[bi, sl, hd * SSM_HEADDIM:(hd + 1) * SSM_HEADDIM]
                    xdt = xh * dt_all[bi, sl, ln:ln + 1]
                    cg_last = cg_col[c - 1:c, :]
                    hst = h_scr[bi, hd]
                    ys.append(_mm(cb * lmat, xdt) + _mm_nt(cm * jnp.exp(cg_col), hst))
                    h_scr[bi, hd] = hst * jnp.exp(cg_last) + _mm(xdt.T, bm * jnp.exp(cg_last - cg_col))
            yc = jnp.concatenate(ys, axis=-1) + dsk_ref[...] * xv[bi, sl]
            yc = yc * _silu(z_ref[rows, :])
            outs = [_rms(yc[:, gi * gw:(gi + 1) * gw], ng[:, gi * gw:(gi + 1) * gw]) for gi in range(SSM_GROUPS)]
            y_ref[rows, :] = jnp.concatenate(outs, axis=-1).astype(y_ref.dtype)

    hfin_ref[...] = h_scr[...]


def _lane_row(vals, offset):
    depth, n = vals.shape
    return jnp.pad(vals.astype(F32), ((0, 0), (offset, LANES - offset - n))).reshape(depth, 1, LANES)


def _ssd(proj, conv_buf, h0, conv_w, conv_b, dt_bias, a_log, d_skip, norm_g, layer, b, t):
    zero_init = h0 is None
    c = math.gcd(t, SSM_CHUNK)
    nc = max(1, min(4, t // c))
    tb = c * nc
    nt = t // tb
    bb = math.gcd(b, 8) if nt == 1 else 1
    rb = bb * tb
    depth = conv_w.shape[0]
    row = lambda bi, i: bi * nt + i
    wblk = SSM_W
    st_dims = (SSM_HEADS, SSM_HEADDIM, SSM_STATE)
    in_specs = [pl.BlockSpec((rb, wblk), lambda bi, i: (row(bi, i), PK_SZ // wblk)),
                pl.BlockSpec((rb, wblk), lambda bi, i: (row(bi, i), PK_SX // wblk)),
                pl.BlockSpec((rb, wblk), lambda bi, i: (row(bi, i), PK_SBC // wblk)),
                pl.BlockSpec((rb, LANES), lambda bi, i: (row(bi, i), PK_SM // LANES)),
                pl.BlockSpec((None, SSM_CONV, wblk), lambda bi, i: (layer, 0, 0)),
                pl.BlockSpec((None, SSM_CONV, wblk), lambda bi, i: (layer, 0, 1)),
                pl.BlockSpec((None, 1, wblk), lambda bi, i: (layer, 0, 0)),
                pl.BlockSpec((None, 1, wblk), lambda bi, i: (layer, 0, 1)),
                pl.BlockSpec((None, 1, LANES), lambda bi, i: (layer, 0, 0)),
                pl.BlockSpec((None, 1, LANES), lambda bi, i: (layer, 0, 0)),
                pl.BlockSpec((None, 1, SSM_W), lambda bi, i: (layer, 0, 0)),
                pl.BlockSpec((None, 1, SSM_W), lambda bi, i: (layer, 0, 0))]
    args = [proj, proj, proj, proj, conv_w, conv_w, conv_b.reshape(depth, 1, -1), conv_b.reshape(depth, 1, -1),
            _lane_row(dt_bias, SM_DT), _lane_row(a_log, SM_DT),
            jnp.repeat(d_skip.astype(F32), SSM_HEADDIM, axis=-1).reshape(depth, 1, SSM_W),
            norm_g.reshape(depth, 1, SSM_W)]
    if not zero_init:
        in_specs += [pl.BlockSpec((None, bb, SSM_CONV - 1, wblk), lambda bi, i: (layer, bi, 0, 0)),
                     pl.BlockSpec((None, bb, SSM_CONV - 1, wblk), lambda bi, i: (layer, bi, 0, 1)),
                     pl.BlockSpec((None, bb) + st_dims, lambda bi, i: (layer, bi, 0, 0, 0))]
        args += [conv_buf, conv_buf, jnp.swapaxes(h0, -1, -2)]
    y, h_fin_t = pl.pallas_call(
        functools.partial(_ssd_kernel, c=c, nc=nc, bb=bb, zero_init=zero_init),
        grid=(b // bb, nt),
        in_specs=in_specs,
        out_specs=[pl.BlockSpec((rb, SSM_W), lambda bi, i: (row(bi, i), 0)),
                   pl.BlockSpec((bb,) + st_dims, lambda bi, i: (bi, 0, 0, 0))],
        out_shape=[jax.ShapeDtypeStruct((b * t, SSM_W), BF16), jax.ShapeDtypeStruct((b,) + st_dims, F32)],
        scratch_shapes=[pltpu.VMEM((bb, tb + SUBLANES, wblk), F32)] * 2 + [pltpu.VMEM((bb,) + st_dims, F32)],
        compiler_params=_cparams(("arbitrary", "arbitrary")),
        name="ssd",
    )(*args)
    return y, jnp.swapaxes(h_fin_t, -1, -2)


def _rope_tables(pos0, t):
    half = SWA_HD // 2
    inv = ROPE_THETA ** (-jnp.arange(half, dtype=F32) / half)
    ang = (pos0 + jnp.arange(t, dtype=jnp.int32)).astype(F32)[:, None] * inv[None, :]
    cos, sin = jnp.cos(ang), jnp.sin(ang)
    return jnp.concatenate([cos, cos], axis=-1), jnp.concatenate([-sin, sin], axis=-1)


def _rope(x, cosf, sinf):
    return x * cosf + pltpu.roll(x, SWA_HD // 2, axis=1) * sinf


def _swa_prompt_kernel(q_ref, k_ref, v_ref, cos_ref, sin_ref, k_prev, v_prev, o_ref, kr_ref, vo_ref,
                       qs, p_num, p_den, p_mx, *, t):
    del k_prev, v_prev
    qb = LANES
    cosf, sinf = cos_ref[...], sin_ref[...]
    qs[...] = _rope(q_ref[...], cosf, sinf) * (SWA_HD ** -0.5)
    kr_ref[...] = _rope(k_ref[...], cosf, sinf)
    vo_ref[...] = v_ref[...]

    m_i = lax.broadcasted_iota(jnp.int32, (qb, 2 * qb), 0)
    n_i = lax.broadcasted_iota(jnp.int32, (qb, 2 * qb), 1)
    mask_two = (n_i >= m_i) & (n_i <= m_i + qb)
    mask_one = (lax.broadcasted_iota(jnp.int32, (qb, qb), 1)
                <= lax.broadcasted_iota(jnp.int32, (qb, qb), 0))

    def block(br, q0, k0, nk, dil):
        if dil == 1:
            qi, ki = pl.ds(q0, qb), pl.ds(k0, nk)
        else:
            qi, ki = pl.ds(q0, qb, stride=dil), pl.ds(k0, nk, stride=dil)
        sc = _mm_nt(qs[qi, :], kr_ref[ki, :])
        sc = jnp.where(mask_one if nk == qb else mask_two, sc, NEG_BIG)
        mx = jnp.max(sc, axis=-1, keepdims=True)
        p = jnp.exp(sc - mx)
        p_num[br, qi, :] = _mm(p, v_ref[ki, :])
        p_den[br, qi, :] = jnp.broadcast_to(jnp.sum(p, axis=-1, keepdims=True), (qb, LANES))
        p_mx[br, qi, :] = jnp.broadcast_to(mx, (qb, LANES))

    for br, (window, dil) in enumerate(DILATIONS):
        assert window == qb * dil and t % (qb * dil) == 0
        nblk = t // (qb * dil)

        def residue(r, carry, br=br, dil=dil, nblk=nblk):
            block(br, r, r, qb, dil)
            if nblk > 1:
                def later(j, cc):
                    block(br, r + dil * qb * j, r + dil * qb * (j - 1), 2 * qb, dil)
                    return cc
                lax.fori_loop(1, nblk, later, 0, unroll=5 if (nblk - 1) % 5 == 0 else 3)
            return carry

        if dil == 1:
            residue(0, 0)
        else:
            lax.fori_loop(0, dil, residue, 0, unroll=8 if nblk == 1 else 2)

    rows = 2 * qb

    def merge(i, carry):
        sl = pl.ds(pl.multiple_of(i * rows, rows), rows)
        mxs = [p_mx[br, sl, :] for br in range(len(DILATIONS))]
        mx = functools.reduce(jnp.maximum, mxs)
        wts = [jnp.exp(m - mx) for m in mxs]
        num = sum(p_num[br, sl, :] * wts[br] for br in range(len(DILATIONS)))
        den = sum(p_den[br, sl, :] * wts[br] for br in range(len(DILATIONS)))
        o_ref[sl, :] = (num / den).astype(o_ref.dtype)
        return carry

    lax.fori_loop(0, t // rows, merge, 0)


def _swa_prompt(proj, layer, k_all, v_all, b, t):
    cosf, sinf = _rope_tables(0, t)
    hq, hk, hv = PK_AQ // LANES, PK_AK // LANES, PK_AV // LANES
    kv_spec = pl.BlockSpec((None, None, None, t, SWA_HD), lambda bi, h: (layer, bi, h, 0, 0))
    hbm = pl.BlockSpec(memory_space=pl.ANY)
    return pl.pallas_call(
        functools.partial(_swa_prompt_kernel, t=t),
        grid=(b, SWA_HEADS),
        in_specs=[pl.BlockSpec((t, LANES), lambda bi, h: (bi, hq + h)),
                  pl.BlockSpec((t, LANES), lambda bi, h: (bi, hk + h)),
                  pl.BlockSpec((t, LANES), lambda bi, h: (bi, hv + h)),
                  pl.BlockSpec((t, LANES), lambda bi, h: (0, 0)),
                  pl.BlockSpec((t, LANES), lambda bi, h: (0, 0)),
                  hbm, hbm],
        out_specs=[pl.BlockSpec((t, LANES), lambda bi, h: (bi, h)), kv_spec, kv_spec],
        out_shape=[jax.ShapeDtypeStruct((b * t, SWA_W), BF16),
                   jax.ShapeDtypeStruct(k_all.shape, F32), jax.ShapeDtypeStruct(v_all.shape, F32)],
        input_output_aliases={5: 1, 6: 2},
        scratch_shapes=[pltpu.VMEM((t, LANES), F32)] + [pltpu.VMEM((len(DILATIONS), t, LANES), F32)] * 3,
        compiler_params=_cparams(("arbitrary", "arbitrary")),
        name="swa_prompt",
    )(proj, proj, proj, cosf, sinf, k_all, v_all)


def _swa_sample_kernel(q_ref, k_ref, v_ref, cos_ref, sin_ref, kf, vf, kn, vn, o_ref, kr_ref, *, t, past):
    (w1, d1), (w2, d2), (w3, d3) = DILATIONS
    n3 = kf.shape[1] * kf.shape[2]
    n_l = kn.shape[1]
    assert d1 == 1 and t <= d3 and w3 == past and n_l == w2 and w1 <= n_l
    assert t & (t - 1) == 0 and d2 & (d2 - 1) == 0
    cosf, sinf = cos_ref[...], sin_ref[...]

    m3 = lax.broadcasted_iota(jnp.int32, (t, n3), 0)
    c3 = lax.broadcasted_iota(jnp.int32, (t, n3), 1)
    mask3 = (c3 & (t - 1)) == m3
    ml = lax.broadcasted_iota(jnp.int32, (t, n_l), 0)
    cl = lax.broadcasted_iota(jnp.int32, (t, n_l), 1)
    dist_l = n_l + ml - cl
    mask1l = dist_l <= w1
    mask2l = (dist_l <= w2) & ((dist_l & (d2 - 1)) == 0)
    mn = lax.broadcasted_iota(jnp.int32, (t, t), 0)
    cn = lax.broadcasted_iota(jnp.int32, (t, t), 1)
    dist_n = mn - cn
    mask1n = dist_n >= 0
    mask2n = (dist_n >= 0) & ((dist_n & (d2 - 1)) == 0)
    mask3n = dist_n == 0

    for h in range(SWA_HEADS):
        hs = slice(h * SWA_HD, (h + 1) * SWA_HD)
        qh = _rope(q_ref[:, hs], cosf, sinf) * (SWA_HD ** -0.5)
        k_new = _rope(k_ref[:, hs], cosf, sinf)
        kr_ref[:, hs] = k_new
        v_new = v_ref[:, hs]
        k3 = kf[h].reshape(n3, SWA_HD)
        v3 = vf[h].reshape(n3, SWA_HD)
        kl = kn[h]
        vl = vn[h]
        s3 = _mm_nt(qh, k3)
        s_l = _mm_nt(qh, kl)
        s_n = _mm_nt(qh, k_new)

        def branch(parts):
            mx = None
            for sc, mask in parts:
                cur = jnp.max(jnp.where(mask, sc, NEG_BIG), axis=-1, keepdims=True)
                mx = cur if mx is None else jnp.maximum(mx, cur)
            ps = [jnp.where(mask, jnp.exp(jnp.where(mask, sc, NEG_BIG) - mx), 0.0) for sc, mask in parts]
            den = sum(jnp.sum(p, axis=-1, keepdims=True) for p in ps)
            return mx, ps, den

        mx1, (p1l, p1n), den1 = branch([(s_l, mask1l), (s_n, mask1n)])
        mx2, (p2l, p2n), den2 = branch([(s_l, mask2l), (s_n, mask2n)])
        mx3, (p3c, p3n), den3 = branch([(s3, mask3), (s_n, mask3n)])
        mx = jnp.maximum(jnp.maximum(mx1, mx2), mx3)
        wt1, wt2, wt3 = jnp.exp(mx1 - mx), jnp.exp(mx2 - mx), jnp.exp(mx3 - mx)
        num = (_mm(p1l * wt1 + p2l * wt2, vl) + _mm(p3c * wt3, v3)
               + _mm(p1n * wt1 + p2n * wt2 + p3n * wt3, v_new))
        den = den1 * wt1 + den2 * wt2 + den3 * wt3
        o_ref[:, hs] = (num / den).astype(o_ref.dtype)


def _swa_sample(proj, cache_k, cache_v, layer, b, t, past):
    depth, _, l_cache, nh, hd = cache_k.shape
    assert l_cache == past and nh * hd == SWA_W
    (_, _), (w2, _), (_, d3) = DILATIONS
    cosf, sinf = _rope_tables(past, t)
    assert l_cache % d3 == 0 and l_cache % w2 == 0 and t % SUBLANES == 0
    ckh = jnp.swapaxes(cache_k, 2, 3)
    cvh = jnp.swapaxes(cache_v, 2, 3)
    far = pl.BlockSpec((None, None, nh, l_cache // d3, t, hd), lambda bi: (layer, bi, 0, 0, 0, 0))
    near = pl.BlockSpec((None, None, nh, w2, hd), lambda bi: (layer, bi, 0, l_cache // w2 - 1, 0))
    split = lambda c: c.reshape(depth, b, nh, l_cache // d3, d3, hd)
    return pl.pallas_call(
        functools.partial(_swa_sample_kernel, t=t, past=past),
        grid=(b,),
        in_specs=[pl.BlockSpec((t, SWA_W), lambda bi: (bi, PK_AQ // SWA_W)),
                  pl.BlockSpec((t, SWA_W), lambda bi: (bi, PK_AK // SWA_W)),
                  pl.BlockSpec((t, SWA_W), lambda bi: (bi, PK_AV // SWA_W)),
                  pl.BlockSpec((t, LANES), lambda bi: (0, 0)),
                  pl.BlockSpec((t, LANES), lambda bi: (0, 0)),
                  far, far, near, near],
        out_specs=[pl.BlockSpec((t, SWA_W), lambda bi: (bi, 0)),
                   pl.BlockSpec((t, SWA_W), lambda bi: (bi, 0))],
        out_shape=[jax.ShapeDtypeStruct((b * t, SWA_W), BF16),
                   jax.ShapeDtypeStruct((b * t, SWA_W), F32)],
        compiler_params=_cparams(("arbitrary",)),
        name="swa_sample",
    )(proj, proj, proj, cosf, sinf, split(ckh), split(cvh), ckh, cvh)


def _outproj_kernel(oa_ref, ob_ref, oc_ref, w_ref, x_ref, gate_ref, gpost_ref, gpre_ref, sh_ref, sc_ref,
                    xo_ref, h2_ref):
    mix = (jnp.dot(oa_ref[...], w_ref[0:GDN_W, :], preferred_element_type=F32)
           + jnp.dot(ob_ref[...], w_ref[GDN_W:GDN_W + SSM_W, :], preferred_element_type=F32)
           + jnp.dot(oc_ref[...], w_ref[GDN_W + SSM_W:, :], preferred_element_type=F32))
    y = _rms(mix, gpost_ref[...])
    x = x_ref[...] + gate_ref[...] * y.reshape(x_ref.shape)
    xo_ref[...] = x
    h2 = _rms(x, gpre_ref[...]) * (1.0 + sc_ref[...]) + sh_ref[...]
    h2_ref[...] = h2.reshape(h2_ref.shape).astype(h2_ref.dtype)


def _outproj(oa, ob, oc, w, x, mod, gpost, gpre, layer, bb, tt):
    b, t, d = x.shape
    nt = t // tt
    rows = bb * tt
    rmap = lambda bi, ti: (bi * nt + ti, 0)
    mspec = lambda k: pl.BlockSpec((None, bb, 1, d), lambda bi, ti: (layer, bi, 0, k))
    gspec = pl.BlockSpec((None, 1, d), lambda bi, ti: (layer, 0, 0))
    return pl.pallas_call(
        _outproj_kernel,
        grid=(b // bb, nt),
        in_specs=[pl.BlockSpec((rows, GDN_W), rmap), pl.BlockSpec((rows, SSM_W), rmap),
                  pl.BlockSpec((rows, SWA_W), rmap),
                  pl.BlockSpec((None, d, d), lambda bi, ti: (layer, 0, 0)),
                  pl.BlockSpec((bb, tt, d), lambda bi, ti: (bi, ti, 0)),
                  mspec(2), gspec, gspec, mspec(3), mspec(4)],
        out_specs=[pl.BlockSpec((bb, tt, d), lambda bi, ti: (bi, ti, 0)),
                   pl.BlockSpec((rows, d), rmap)],
        out_shape=[jax.ShapeDtypeStruct((b, t, d), F32), jax.ShapeDtypeStruct((b * t, d), BF16)],
        compiler_params=_cparams(("arbitrary", "arbitrary")),
        name="outproj",
    )(oa, ob, oc, w, x, mod, gpost, gpre, mod, mod)


def _ffn_up_kernel(*refs, tt, zero_init):
    if zero_init:
        (h_ref, wg_ref, wu_ref, cwg_ref, cwu_ref, cbg_ref, cbu_ref, act_ref, nsg_ref, nsu_ref, csg, csu) = refs
        stg_ref = stu_ref = None
    else:
        (h_ref, wg_ref, wu_ref, cwg_ref, cwu_ref, cbg_ref, cbu_ref, stg_ref, stu_ref,
         act_ref, nsg_ref, nsu_ref, csg, csu) = refs
    first = pl.program_id(2) == 0
    lo = SUBLANES - (FFN_CONV - 1)
    bb, _, tn = csg.shape

    for st_ref, cs in ((stg_ref, csg), (stu_ref, csu)):
        @pl.when(first)
        def _(st_ref=st_ref, cs=cs):
            if st_ref is None:
                cs[:, lo:SUBLANES, :] = jnp.zeros((bb, FFN_CONV - 1, tn), F32)
            else:
                cs[:, lo:SUBLANES, :] = st_ref[...]

        @pl.when(jnp.logical_not(first))
        def _(cs=cs):
            cs[:, lo:SUBLANES, :] = cs[:, tt + lo:tt + SUBLANES, :]

    h = h_ref[...]
    sub = 2 * LANES

    def half(w_ref, cw_ref, cb_ref, ns_ref, cs, cols):
        up = jnp.dot(h, w_ref[:, cols], preferred_element_type=F32)
        cs[:, SUBLANES:SUBLANES + tt, cols] = up.reshape(bb, tt, up.shape[-1])
        y = cs[:, lo:lo + tt, cols] * cw_ref[0:1, cols]
        for j in range(1, FFN_CONV):
            y = y + cs[:, lo + j:lo + j + tt, cols] * cw_ref[j:j + 1, cols]
        ns_ref[:, :, cols] = cs[:, tt + lo:tt + SUBLANES, cols]
        return y + cb_ref[:, cols]

    for c0 in range(0, tn, sub):
        cols = slice(c0, min(c0 + sub, tn))
        yg = half(wg_ref, cwg_ref, cbg_ref, nsg_ref, csg, cols)
        yu = half(wu_ref, cwu_ref, cbu_ref, nsu_ref, csu, cols)
        act = _silu(yg) * yu
        act_ref[:, cols] = act.reshape(bb * tt, act.shape[-1]).astype(act_ref.dtype)


def _ffn_up(h2, state, w_up, conv_w, conv_b, layer, b, t, bb, tt):
    zero_init = state is None
    d = h2.shape[1]
    tn = 1408
    nh = D_FF // tn
    nt = t // tt
    rows = bb * tt
    depth = conv_w.shape[0]
    cb = conv_b.reshape(depth, 1, -1)
    in_specs = [pl.BlockSpec((rows, d), lambda bi, j, ti: (bi * nt + ti, 0)),
                pl.BlockSpec((None, d, tn), lambda bi, j, ti: (layer, 0, j)),
                pl.BlockSpec((None, d, tn), lambda bi, j, ti: (layer, 0, nh + j)),
                pl.BlockSpec((None, FFN_CONV, tn), lambda bi, j, ti: (layer, 0, j)),
                pl.BlockSpec((None, FFN_CONV, tn), lambda bi, j, ti: (layer, 0, nh + j)),
                pl.BlockSpec((None, 1, tn), lambda bi, j, ti: (layer, 0, j)),
                pl.BlockSpec((None, 1, tn), lambda bi, j, ti: (layer, 0, nh + j))]
    args = [h2, w_up, w_up, conv_w, conv_w, cb, cb]
    if not zero_init:
        in_specs += [pl.BlockSpec((None, bb, FFN_CONV - 1, tn), lambda bi, j, ti: (layer, bi, 0, j)),
                     pl.BlockSpec((None, bb, FFN_CONV - 1, tn), lambda bi, j, ti: (layer, bi, 0, nh + j))]
        args += [state, state]
    ns_spec = pl.BlockSpec((bb, FFN_CONV - 1, tn), lambda bi, j, ti: (bi, 0, j))
    act, nsg, nsu = pl.pallas_call(
        functools.partial(_ffn_up_kernel, tt=tt, zero_init=zero_init),
        grid=(b // bb, nh, nt),
        in_specs=in_specs,
        out_specs=[pl.BlockSpec((rows, tn), lambda bi, j, ti: (bi * nt + ti, j)), ns_spec, ns_spec],
        out_shape=[jax.ShapeDtypeStruct((b * t, D_FF), BF16),
                   jax.ShapeDtypeStruct((b, FFN_CONV - 1, D_FF), F32),
                   jax.ShapeDtypeStruct((b, FFN_CONV - 1, D_FF), F32)],
        scratch_shapes=[pltpu.VMEM((bb, tt + SUBLANES, tn), F32)] * 2,
        compiler_params=_cparams(("arbitrary", "arbitrary", "arbitrary")),
        name="ffn_up",
    )(*args)
    return act, jnp.concatenate([nsg, nsu], axis=-1)


def _ffn_down_kernel(a_ref, w_ref, x_ref, gate_ref, g_ref, o_ref):
    k = pl.program_id(2)
    part = lambda: jnp.dot(a_ref[...], w_ref[...], preferred_element_type=F32).reshape(o_ref.shape)

    @pl.when(k == 0)
    def _():
        o_ref[...] = part()

    @pl.when(k > 0)
    def _():
        o_ref[...] += part()

    @pl.when(k == pl.num_programs(2) - 1)
    def _():
        o_ref[...] = x_ref[...] + gate_ref[...] * _rms(o_ref[...], g_ref[...])


def _ffn_down(act, w_down, x, mod, g, layer, bb, tt):
    b, t, d = x.shape
    nt = t // tt
    rows = bb * tt
    tk = 512
    return pl.pallas_call(
        _ffn_down_kernel,
        grid=(b // bb, nt, D_FF // tk),
        in_specs=[pl.BlockSpec((rows, tk), lambda bi, ti, k: (bi * nt + ti, k)),
                  pl.BlockSpec((None, tk, d), lambda bi, ti, k: (layer, k, 0)),
                  pl.BlockSpec((bb, tt, d), lambda bi, ti, k: (bi, ti, 0)),
                  pl.BlockSpec((None, bb, 1, d), lambda bi, ti, k: (layer, bi, 0, 5)),
                  pl.BlockSpec((None, 1, d), lambda bi, ti, k: (layer, 0, 0))],
        out_specs=pl.BlockSpec((bb, tt, d), lambda bi, ti, k: (bi, ti, 0)),
        out_shape=jax.ShapeDtypeStruct((b, t, d), F32),
        compiler_params=_cparams(("arbitrary", "arbitrary", "arbitrary")),
        name="ffn_down",
    )(act, w_down, x, mod, g)


def _pack_w_in(w_in):
    depth, d, _ = w_in.shape
    o_gb = 4 * GDN_W
    o_sz = o_gb + 2 * GDN_HEADS
    o_dt = o_sz + SSM_W + SSM_CONV_CH
    o_aq = o_dt + SSM_HEADS
    pad = jnp.zeros((depth, d, LANES - 2 * GDN_HEADS - SSM_HEADS), w_in.dtype)
    packed = jnp.concatenate([w_in[..., :o_gb], w_in[..., o_sz:o_dt], w_in[..., o_aq:],
                              w_in[..., o_gb:o_sz], w_in[..., o_dt:o_aq], pad], axis=-1)
    assert packed.shape[-1] == PK_COLS
    return packed.astype(BF16)


def _layer(x, mod, layer, states, kv_all, wts, past, bb, tt):
    b, t, d = x.shape
    if states is None:
        gdn_conv = gdn_s = ssm_conv = ssm_h = cache_k = cache_v = ffn_conv = None
    else:
        gdn_conv, gdn_s, ssm_conv, ssm_h, cache_k, cache_v, ffn_conv = states
    tt2 = 2 * tt if (bb == 1 and t % (2 * tt) == 0) else tt
    proj = _inproj(x, mod, wts["g_pre_mix"], wts["w_in"], layer, bb, tt2)
    gdn_w = (wts["gdn_conv_w"], wts["gdn_a_log"], wts["gdn_dt_bias"], wts["gdn_norm_g"])
    if states is None:
        o_a, new_gdn_s = _gdn_seq(proj, *gdn_w, layer, b, t)
    else:
        o_a, new_gdn_s = _gdn_step(proj, gdn_conv, gdn_s, *gdn_w, layer, b, t)
    o_b, new_ssm_h = _ssd(proj, ssm_conv, ssm_h, wts["ssm_conv_w"], wts["ssm_conv_b"], wts["ssm_dt_bias"],
                          wts["ssm_a_log"], wts["ssm_d"], wts["ssm_norm_g"], layer, b, t)
    if states is None:
        o_c, *kv_all = _swa_prompt(proj, layer, *kv_all, b, t)
    else:
        o_c, k_rot = _swa_sample(proj, cache_k, cache_v, layer, b, t, past)
    x_mid, h2 = _outproj(o_a, o_b, o_c, wts["w_out"], x, mod, wts["g_post_mix"], wts["g_pre_ffn"], layer, bb, tt)
    act, new_ffn_conv = _ffn_up(h2, ffn_conv, wts["w_up"], wts["ffn_conv_w"], wts["ffn_conv_b"], layer, b, t, bb, tt2)
    x_out = _ffn_down(act, wts["w_down"], x_mid, mod, wts["g_post_ffn"], layer, bb, tt2)

    proj3 = proj.reshape(b, t, PK_COLS)
    new_gdn_conv = proj3[:, t - (GDN_CONV - 1):, PK_GQ:PK_GQ + 3 * GDN_W]
    new_ssm_conv = proj3[:, t - (SSM_CONV - 1):, PK_SX:PK_SX + SSM_CONV_CH]
    outs = [new_gdn_conv, new_gdn_s, new_ssm_conv, new_ssm_h, new_ffn_conv]
    if states is not None:
        outs += [k_rot.reshape(b, t, SWA_HEADS, SWA_HD), proj3[:, :, PK_AV:PK_AV + SWA_W].reshape(b, t, SWA_HEADS, SWA_HD)]
    return x_out, outs, kv_all


def kernel(x_prompt, x_sample, c_prompt, c_sample, state_gdn_conv, state_gdn, state_ssm_conv, state_ssm, cache_k, cache_v, state_ffn_conv, w_ada, b_ada, g_pre_mix, g_post_mix, g_pre_ffn, g_post_ffn, w_in, gdn_conv_w, gdn_a_log, gdn_dt_bias, gdn_norm_g, ssm_conv_w, ssm_conv_b, ssm_dt_bias, ssm_a_log, ssm_d, ssm_norm_g, w_out, w_up, ffn_conv_w, ffn_conv_b, w_down):
    depth = w_ada.shape[0]
    bp, tp, d = x_prompt.shape
    bs, ts, _ = x_sample.shape
    past = cache_k.shape[2]
    assert tp >= GDN_CONV and ts >= GDN_CONV and ts % SUBLANES == 0

    vec = lambda a: a.reshape(depth, 1, a.shape[-1])
    wts = dict(
        g_pre_mix=vec(g_pre_mix), g_post_mix=vec(g_post_mix), g_pre_ffn=vec(g_pre_ffn), g_post_ffn=vec(g_post_ffn),
        w_in=_pack_w_in(w_in), w_out=w_out.astype(BF16), w_up=w_up.astype(BF16), w_down=w_down.astype(BF16),
        gdn_conv_w=gdn_conv_w, gdn_a_log=gdn_a_log, gdn_dt_bias=gdn_dt_bias, gdn_norm_g=gdn_norm_g,
        ssm_conv_w=ssm_conv_w, ssm_conv_b=ssm_conv_b, ssm_dt_bias=ssm_dt_bias, ssm_a_log=ssm_a_log, ssm_d=ssm_d,
        ssm_norm_g=ssm_norm_g, ffn_conv_w=ffn_conv_w, ffn_conv_b=ffn_conv_b)

    mod = _ada(jnp.concatenate([c_prompt, c_sample], axis=0), w_ada, b_ada)
    mod_p = mod[:, :bp].reshape(depth, bp, 1, 6 * d)
    mod_s = mod[:, bp:].reshape(depth, bs, 1, 6 * d)
    s_states = (state_gdn_conv, state_gdn, state_ssm_conv, state_ssm, cache_k, cache_v, state_ffn_conv)

    tt_p = math.gcd(tp, 512)
    bb_s = math.gcd(bs, 512 // ts)
    assert tp <= W_MAX and ts <= W_MAX
    y_p, y_s = x_prompt, x_sample
    kv_p = [jnp.zeros((depth, bp, SWA_HEADS, tp, SWA_HD), F32) for _ in range(2)]
    p_rows, s_rows = [], []
    for layer in range(depth):
        y_p, outs_p, kv_p = _layer(y_p, mod_p, layer, None, kv_p, wts, 0, 1, tt_p)
        y_s, outs_s, _ = _layer(y_s, mod_s, layer, s_states, None, wts, past, bb_s, ts)
        p_rows.append(outs_p)
        s_rows.append(outs_s)
    p_gdn_conv, p_gdn, p_ssm_conv, p_ssm, p_ffn_conv = [jnp.stack(a) for a in zip(*p_rows)]
    s_gdn_conv, s_gdn, s_ssm_conv, s_ssm, s_ffn_conv, s_k, s_v = [jnp.stack(a) for a in zip(*s_rows)]
    p_k, p_v = [jnp.swapaxes(a, 2, 3) for a in kv_p]
    return (y_p, y_s, p_gdn_conv, p_gdn, p_ssm_conv, p_ssm, p_k, p_v, p_ffn_conv,
            s_gdn_conv, s_gdn, s_ssm_conv, s_ssm, s_k, s_v, s_ffn_conv)
```

```python
import functools
import math

import jax
import jax.numpy as jnp
import numpy as np
from jax import lax
from jax.experimental import pallas as pl
from jax.experimental.pallas import tpu as pltpu

F32 = jnp.float32
BF16 = jnp.bfloat16

D_MODEL = 2048
MIX_UNIT = D_MODEL // 8
GDN_W = 3 * MIX_UNIT
SSM_W = 2 * MIX_UNIT
SWA_W = 3 * MIX_UNIT
GDN_DK = 128
GDN_DV = 128
GDN_HEADS = GDN_W // GDN_DV
GDN_CONV = 4
GDN_CHUNK = 64
SSM_HEADDIM = 64
SSM_HEADS = SSM_W // SSM_HEADDIM
SSM_GROUPS = 2
SSM_STATE = 128
SSM_CONV = 4
SSM_CHUNK = 64
SSM_CONV_CH = SSM_W + 2 * SSM_GROUPS * SSM_STATE
SWA_HD = 128
SWA_HEADS = SWA_W // SWA_HD
DILATIONS = ((128, 1), (512, 4), (2048, 16))
W_MAX = 2048
ROPE_THETA = 10000.0
D_FF = 11 * D_MODEL // 4
FFN_CONV = 3
NORM_EPS = 1e-6

LANES = 128
SUBLANES = 8
VMEM_LIMIT = 56 * 1024 * 1024

PK_GQ = 0
PK_GK = PK_GQ + GDN_W
PK_GV = PK_GK + GDN_W
PK_GZ = PK_GV + GDN_W
PK_SZ = PK_GZ + GDN_W
PK_SX = PK_SZ + SSM_W
PK_SBC = PK_SX + SSM_W
PK_AQ = PK_SX + SSM_CONV_CH
PK_AK = PK_AQ + SWA_W
PK_AV = PK_AK + SWA_W
PK_SM = PK_AV + SWA_W
PK_COLS = PK_SM + LANES
SM_BETA = 0
SM_A = GDN_HEADS
SM_DT = 2 * GDN_HEADS
NEG_BIG = -1e30
GDN_SEQ_HEADS = 3
GDN_SEQ_CHUNKS = 8


def _cparams(sem):
    return pltpu.CompilerParams(dimension_semantics=sem, vmem_limit_bytes=VMEM_LIMIT)


def _sigmoid(x):
    return 1.0 / (1.0 + jnp.exp(-x))


def _silu(x):
    return x * _sigmoid(x)


def _softplus(x):
    return jnp.maximum(x, 0.0) + jnp.log1p(jnp.exp(-jnp.abs(x)))


def _mm(a, b):
    return jnp.dot(a.astype(BF16), b.astype(BF16), preferred_element_type=F32)


def _mm_nt(a, b):
    return lax.dot_general(a.astype(BF16), b.astype(BF16), (((1,), (1,)), ((), ())), preferred_element_type=F32)


def _mm_hi(a, b):
    return jnp.dot(a, b, preferred_element_type=F32, precision=lax.Precision.HIGHEST)


def _rms(x, g):
    return x * lax.rsqrt(jnp.mean(x * x, axis=-1, keepdims=True) + NORM_EPS) * g


def _ada_kernel(c_ref, w_ref, b_ref, o_ref):
    a = _silu(c_ref[...]).astype(BF16)
    o_ref[...] = jnp.dot(a, w_ref[...].astype(BF16), preferred_element_type=F32) + b_ref[...]


def _ada(c_all, w_ada, b_ada):
    depth, d, n = w_ada.shape
    r = c_all.shape[0]
    tn = 1024
    return pl.pallas_call(
        _ada_kernel,
        grid=(depth, n // tn),
        in_specs=[pl.BlockSpec((r, d), lambda l, j: (0, 0)),
                  pl.BlockSpec((None, d, tn), lambda l, j: (l, 0, j)),
                  pl.BlockSpec((None, 1, tn), lambda l, j: (l, 0, j))],
        out_specs=pl.BlockSpec((None, r, tn), lambda l, j: (l, 0, j)),
        out_shape=jax.ShapeDtypeStruct((depth, r, n), F32),
        compiler_params=_cparams(("arbitrary", "arbitrary")),
        name="ada",
    )(c_all, w_ada, b_ada.reshape(depth, 1, n))


def _inproj_kernel(x_ref, sh_ref, sc_ref, g_ref, w_ref, o_ref, h_scr):
    @pl.when(pl.program_id(2) == 0)
    def _():
        h = _rms(x_ref[...], g_ref[...]) * (1.0 + sc_ref[...]) + sh_ref[...]
        h_scr[...] = h.reshape(h_scr.shape).astype(BF16)

    o_ref[...] = jnp.dot(h_scr[...], w_ref[...], preferred_element_type=F32)


def _inproj(x, mod, g, w, layer, bb, tt):
    b, t, d = x.shape
    n = w.shape[-1]
    tn = 1408
    nt = t // tt
    return pl.pallas_call(
        _inproj_kernel,
        grid=(b // bb, nt, n // tn),
        in_specs=[pl.BlockSpec((bb, tt, d), lambda bi, ti, j: (bi, ti, 0)),
                  pl.BlockSpec((None, bb, 1, d), lambda bi, ti, j: (layer, bi, 0, 0)),
                  pl.BlockSpec((None, bb, 1, d), lambda bi, ti, j: (layer, bi, 0, 1)),
                  pl.BlockSpec((None, 1, d), lambda bi, ti, j: (layer, 0, 0)),
                  pl.BlockSpec((None, d, tn), lambda bi, ti, j: (layer, 0, j))],
        out_specs=pl.BlockSpec((bb * tt, tn), lambda bi, ti, j: (bi * nt + ti, j)),
        out_shape=jax.ShapeDtypeStruct((b * t, n), F32),
        scratch_shapes=[pltpu.VMEM((bb * tt, d), BF16)],
        compiler_params=_cparams(("arbitrary", "arbitrary", "arbitrary")),
        name="inproj",
    )(x, mod, mod, g, w)


def _conv_from_scratch(xs, w_ref, width, t):
    lo = SUBLANES - (width - 1)
    y = xs[lo:lo + t, :] * w_ref[0:1, :]
    for j in range(1, width):
        y = y + xs[lo + j:lo + j + t, :] * w_ref[j:j + 1, :]
    return y


def _conv_stage(xs, x_new, hist_ref, first, width, t):
    lo = SUBLANES - (width - 1)

    @pl.when(first)
    def _():
        if hist_ref is None:
            xs[lo:SUBLANES, :] = jnp.zeros((width - 1, xs.shape[1]), F32)
        else:
            xs[lo:SUBLANES, :] = hist_ref[...]

    @pl.when(jnp.logical_not(first))
    def _():
        xs[lo:SUBLANES, :] = xs[t + lo:t + SUBLANES, :]

    xs[SUBLANES:SUBLANES + t, :] = x_new


def _tri_masks(c):
    row = lax.broadcasted_iota(jnp.int32, (c, c), 0)
    col = lax.broadcasted_iota(jnp.int32, (c, c), 1)
    return row == col, row >= col, row > col, row <= col


def _bdot(a, b, ca, cb):
    return lax.dot_general(a, b, (((ca,), (cb,)), ((0,), (0,))), preferred_element_type=F32)


def _bmm(a, b):
    return _bdot(a.astype(BF16), b.astype(BF16), 2, 1)


def _bmm_nt(a, b):
    return _bdot(a.astype(BF16), b.astype(BF16), 2, 2)


def _bmm_tn(a, b):
    return _bdot(a.astype(BF16), b.astype(BF16), 1, 1)


def _split_bf16(a):
    hi = a.astype(BF16)
    return hi, (a - hi.astype(F32)).astype(BF16)


def _bmm3(a, b):
    ah, al = _split_bf16(a)
    bh, bl = _split_bf16(b)
    return _bdot(ah, bh, 2, 1) + _bdot(ah, bl, 2, 1) + _bdot(al, bh, 2, 1)


def _unit_lower_inverse(a, eye_f, c):
    x = eye_f - a
    p = _bmm3(a, a)
    n = 2
    while True:
        x = x + _bmm3(x, p)
        n *= 2
        if n >= c:
            return x
        p = _bmm3(p, p)


def _gdn_prep(q, k, v, beta, g, c):
    eye, causal, strict, upper = _tri_masks(c)
    g_row = jnp.sum(jnp.where(eye, g, 0.0), axis=1, keepdims=True)
    cg_col = jnp.sum(jnp.where(causal, g_row, 0.0), axis=2, keepdims=True)
    cg_row = jnp.sum(jnp.where(upper, g, 0.0), axis=1, keepdims=True)
    gam = jnp.where(causal, jnp.exp(jnp.where(causal, cg_col - cg_row, 0.0)), 0.0)
    kb = k * beta
    amat = jnp.where(strict, _bmm_nt(kb, k) * gam, 0.0)
    tinv = _unit_lower_inverse(amat, eye.astype(F32), c)
    ecg = jnp.exp(cg_col)
    u = _bmm3(tinv, v * beta)
    w = _bmm3(tinv, kb * ecg)
    qk = _bmm_nt(q, k) * gam
    cg_last = cg_col[:, c - 1:c, :]
    kd = k * jnp.exp(cg_last - cg_col)
    return u, w, qk, q * ecg, kd, jnp.exp(cg_last)


def _gdn_gates(b_raw, a_raw, a_log, dt_bias):
    a_neg = -jnp.exp(jnp.full((1, 1), a_log, F32))
    return _sigmoid(b_raw), a_neg * _softplus(a_raw + dt_bias)


def _l2norm(x):
    return x * lax.rsqrt(jnp.sum(x * x, axis=-1, keepdims=True) + NORM_EPS)


def _gdn_seq_kernel(alog_ref, dtb_ref, q_ref, k_ref, v_ref, z_ref, sm_ref, wq_ref, wk_ref, wv_ref, ng_ref,
                    o_ref, sfin_ref, xq, xk, xv, s_scr, *, c, nc, hp):
    hg = pl.program_id(1)
    tb = c * nc
    first = pl.program_id(2) == 0

    @pl.when(first)
    def _():
        s_scr[...] = jnp.zeros_like(s_scr)

    _conv_stage(xq, q_ref[...], None, first, GDN_CONV, tb)
    _conv_stage(xk, k_ref[...], None, first, GDN_CONV, tb)
    _conv_stage(xv, v_ref[...], None, first, GDN_CONV, tb)
    qa = _silu(_conv_from_scratch(xq, wq_ref, GDN_CONV, tb))
    ka = _silu(_conv_from_scratch(xk, wk_ref, GDN_CONV, tb))
    va = _silu(_conv_from_scratch(xv, wv_ref, GDN_CONV, tb))

    sm = sm_ref[...]
    lane = lax.broadcasted_iota(jnp.int32, sm.shape, 1)
    chunks = lambda a: a.reshape(nc, c, a.shape[-1])
    parts = []
    for j in range(hp):
        h = hg * hp + j
        hs = slice(j * GDN_DK, (j + 1) * GDN_DK)
        b_raw = jnp.sum(jnp.where(lane == SM_BETA + h, sm, 0.0), axis=1, keepdims=True)
        a_raw = jnp.sum(jnp.where(lane == SM_A + h, sm, 0.0), axis=1, keepdims=True)
        beta, g = _gdn_gates(b_raw, a_raw, alog_ref[h], dtb_ref[h])
        parts.append([chunks(a) for a in (_l2norm(qa[:, hs]) * (GDN_DK ** -0.5), _l2norm(ka[:, hs]), va[:, hs],
                                          beta, g)])
    u, w, qk, qg, kd, gl = _gdn_prep(*[jnp.concatenate(a, axis=0) for a in zip(*parts)], c)

    ng = ng_ref[...]
    states = [s_scr[j] for j in range(hp)]
    for ci in range(nc):
        sl = slice(ci * c, (ci + 1) * c)
        for j in range(hp):
            p = j * nc + ci
            hs = slice(j * GDN_DV, (j + 1) * GDN_DV)
            ws = _mm(jnp.concatenate([w[p], qg[p]], axis=0), states[j])
            vnew = u[p] - ws[:c]
            r = _mm(jnp.concatenate([qk[p], kd[p].T], axis=0), vnew)
            states[j] = states[j] * gl[p] + r[c:]
            o = _rms(ws[c:] + r[:c], ng) * _silu(z_ref[sl, hs])
            o_ref[sl, hs] = o.astype(o_ref.dtype)
    for j in range(hp):
        s_scr[j] = states[j]
        sfin_ref[j] = states[j]


def _gdn_seq(proj, conv_w, a_log, dt_bias, norm_g, layer, b, t):
    c = math.gcd(t, GDN_CHUNK)
    nc = math.gcd(t // c, GDN_SEQ_CHUNKS)
    hp = GDN_SEQ_HEADS
    assert GDN_HEADS % hp == 0
    tb = c * nc
    nt = t // tb
    wide = hp * LANES
    hq, hk, hv, hz = PK_GQ // wide, PK_GK // wide, PK_GV // wide, PK_GZ // wide
    smem = pl.BlockSpec(memory_space=pltpu.SMEM)
    cblk = lambda off: pl.BlockSpec((tb, wide), lambda bi, h, i: (bi * nt + i, off + h))
    wblk = lambda off: pl.BlockSpec((None, GDN_CONV, wide), lambda bi, h, i: (layer, 0, off + h))
    return pl.pallas_call(
        functools.partial(_gdn_seq_kernel, c=c, nc=nc, hp=hp),
        grid=(b, GDN_HEADS // hp, nt),
        in_specs=[smem, smem, cblk(hq), cblk(hk), cblk(hv), cblk(hz),
                  pl.BlockSpec((tb, LANES), lambda bi, h, i: (bi * nt + i, PK_SM // LANES)),
                  wblk(0), wblk(GDN_HEADS // hp), wblk(2 * GDN_HEADS // hp),
                  pl.BlockSpec((None, 1, LANES), lambda bi, h, i: (layer, 0, 0))],
        out_specs=[cblk(0), pl.BlockSpec((None, hp, GDN_DK, GDN_DV), lambda bi, h, i: (bi, h, 0, 0))],
        out_shape=[jax.ShapeDtypeStruct((b * t, GDN_W), BF16),
                   jax.ShapeDtypeStruct((b, GDN_HEADS, GDN_DK, GDN_DV), F32)],
        scratch_shapes=[pltpu.VMEM((tb + SUBLANES, wide), F32)] * 3 + [pltpu.VMEM((hp, GDN_DK, GDN_DV), F32)],
        compiler_params=_cparams(("arbitrary", "arbitrary", "arbitrary")),
        name="gdn_seq",
    )(a_log[layer], dt_bias[layer], proj, proj, proj, proj, proj, conv_w, conv_w, conv_w,
      norm_g.reshape(norm_g.shape[0], 1, LANES))


def _gdn_step_kernel(alog_ref, dtb_ref, x_ref, z_ref, sm_ref, w_ref, ng_ref, hist_ref, s0_ref,
                     o_ref, sfin_ref, cs, *, bb, t):
    lo = SUBLANES - (GDN_CONV - 1)
    cs[:, lo:SUBLANES, :] = hist_ref[...]
    cs[:, SUBLANES:SUBLANES + t, :] = x_ref[...].reshape(bb, t, 3 * GDN_W)
    y = cs[:, lo:lo + t, :] * w_ref[0:1, :]
    for j in range(1, GDN_CONV):
        y = y + cs[:, lo + j:lo + j + t, :] * w_ref[j:j + 1, :]
    y = _silu(y)
    sm = sm_ref[...].reshape(bb, t, LANES)
    ng = ng_ref[...]
    for h in range(GDN_HEADS):
        hs = slice(h * GDN_DK, (h + 1) * GDN_DK)
        q = _l2norm(y[:, :, hs]) * (GDN_DK ** -0.5)
        k = _l2norm(y[:, :, GDN_W + h * GDN_DK:GDN_W + (h + 1) * GDN_DK])
        v = y[:, :, 2 * GDN_W + h * GDN_DV:2 * GDN_W + (h + 1) * GDN_DV]
        beta, g = _gdn_gates(sm[:, :, SM_BETA + h:SM_BETA + h + 1], sm[:, :, SM_A + h:SM_A + h + 1],
                             alog_ref[h], dtb_ref[h])
        u, w, qk, qg, kd, gl = _gdn_prep(q, k, v, beta, g, t)
        s = s0_ref[:, h]
        vnew = u - _bmm(w, s)
        o = _bmm(qg, s) + _bmm(qk, vnew)
        sfin_ref[:, h] = s * gl + _bmm_tn(kd, vnew)
        o = _rms(o, ng) * _silu(z_ref[:, hs].reshape(bb, t, GDN_DV))
        o_ref[:, hs] = o.reshape(bb * t, GDN_DV).astype(o_ref.dtype)


def _gdn_step(proj, conv_buf, s0, conv_w, a_log, dt_bias, norm_g, layer, b, t):
    assert t % SUBLANES == 0 and GDN_CHUNK % t == 0 and t >= GDN_CONV - 1
    bb = math.gcd(b, 8)
    rows = bb * t
    smem = pl.BlockSpec(memory_space=pltpu.SMEM)
    return pl.pallas_call(
        functools.partial(_gdn_step_kernel, bb=bb, t=t),
        grid=(b // bb,),
        in_specs=[smem, smem,
                  pl.BlockSpec((rows, 3 * GDN_W), lambda bi: (bi, PK_GQ // (3 * GDN_W))),
                  pl.BlockSpec((rows, GDN_W), lambda bi: (bi, PK_GZ // GDN_W)),
                  pl.BlockSpec((rows, LANES), lambda bi: (bi, PK_SM // LANES)),
                  pl.BlockSpec((None, GDN_CONV, 3 * GDN_W), lambda bi: (layer, 0, 0)),
                  pl.BlockSpec((None, 1, LANES), lambda bi: (layer, 0, 0)),
                  pl.BlockSpec((None, bb, GDN_CONV - 1, 3 * GDN_W), lambda bi: (layer, bi, 0, 0)),
                  pl.BlockSpec((None, bb, GDN_HEADS, GDN_DK, GDN_DV), lambda bi: (layer, bi, 0, 0, 0))],
        out_specs=[pl.BlockSpec((rows, GDN_W), lambda bi: (bi, 0)),
                   pl.BlockSpec((bb, GDN_HEADS, GDN_DK, GDN_DV), lambda bi: (bi, 0, 0, 0))],
        out_shape=[jax.ShapeDtypeStruct((b * t, GDN_W), BF16),
                   jax.ShapeDtypeStruct((b, GDN_HEADS, GDN_DK, GDN_DV), F32)],
        scratch_shapes=[pltpu.VMEM((bb, t + SUBLANES, 3 * GDN_W), F32)],
        compiler_params=_cparams(("arbitrary",)),
        name="gdn_step",
    )(a_log[layer], dt_bias[layer], proj, proj, proj, conv_w, norm_g.reshape(norm_g.shape[0], 1, LANES),
      conv_buf, s0)


def _ssd_kernel(*refs, c, nc, bb, zero_init):
    if zero_init:
        (z_ref, x_ref, bc_ref, sm_ref, wx_ref, wbc_ref, bx_ref, bbc_ref, dtb_ref, alog_ref, dsk_ref, ng_ref,
         y_ref, hfin_ref, xs, xbc, h_scr) = refs
        cx_ref = cbc_ref = h0_ref = None
    else:
        (z_ref, x_ref, bc_ref, sm_ref, wx_ref, wbc_ref, bx_ref, bbc_ref, dtb_ref, alog_ref, dsk_ref, ng_ref,
         cx_ref, cbc_ref, h0_ref, y_ref, hfin_ref, xs, xbc, h_scr) = refs
    tb = c * nc
    first = pl.program_id(1) == 0
    lo = SUBLANES - (SSM_CONV - 1)

    @pl.when(first)
    def _():
        if zero_init:
            h_scr[...] = jnp.zeros_like(h_scr)
        else:
            h_scr[...] = h0_ref[...]

    def conv(cs, new_ref, hist_ref, w_ref, b_ref):
        width = cs.shape[2]

        @pl.when(first)
        def _():
            if hist_ref is None:
                cs[:, lo:SUBLANES, :] = jnp.zeros((bb, SSM_CONV - 1, width), F32)
            else:
                cs[:, lo:SUBLANES, :] = hist_ref[...]

        @pl.when(jnp.logical_not(first))
        def _():
            cs[:, lo:SUBLANES, :] = cs[:, tb + lo:tb + SUBLANES, :]

        cs[:, SUBLANES:SUBLANES + tb, :] = new_ref[...].reshape(bb, tb, width)
        y = cs[:, lo:lo + tb, :] * w_ref[0:1, :]
        for j in range(1, SSM_CONV):
            y = y + cs[:, lo + j:lo + j + tb, :] * w_ref[j:j + 1, :]
        return _silu(y + b_ref[...])

    xv = conv(xs, x_ref, cx_ref, wx_ref, bx_ref)
    bcv = conv(xbc, bc_ref, cbc_ref, wbc_ref, bbc_ref)
    gs = SSM_GROUPS * SSM_STATE

    dt_all = _softplus(sm_ref[...] + dtb_ref[...]).reshape(bb, tb, LANES)
    da_all = dt_all * (-jnp.exp(alog_ref[...]))
    _, causal, _, _ = _tri_masks(c)
    tril_f = causal.astype(F32)
    rep = SSM_HEADS // SSM_GROUPS
    gw = SSM_W // SSM_GROUPS
    ng = ng_ref[...]
    for bi in range(bb):
        for ci in range(nc):
            sl = slice(ci * c, (ci + 1) * c)
            rows = slice(bi * tb + ci * c, bi * tb + (ci + 1) * c)
            cg_all = _mm_hi(tril_f, da_all[bi, sl])
            cg_t = cg_all.T
            ys = []
            for grp in range(SSM_GROUPS):
                bm = bcv[bi, sl, grp * SSM_STATE:(grp + 1) * SSM_STATE]
                cm = bcv[bi, sl, gs + grp * SSM_STATE:gs + (grp + 1) * SSM_STATE]
                cb = _mm_nt(cm, bm)
                for hh in range(rep):
                    hd = grp * rep + hh
                    ln = SM_DT + hd
                    cg_col = cg_all[:, ln:ln + 1]
                    cg_row = cg_t[ln:ln + 1, :]
                    lmat = jnp.where(causal, jnp.exp(jnp.where(causal, cg_col - cg_row, 0.0)), 0.0)
                    xh = xv[bi, sl, hd * SSM_HEADDIM:(hd + 1) * SSM_HEADDIM]
                    xdt = xh * dt_all[bi, sl, ln:ln + 1]
                    cg_last = cg_col[c - 1:c, :]
                    hst = h_scr[bi, hd]
                    ys.append(_mm(cb * lmat, xdt) + _mm_nt(cm * jnp.exp(cg_col), hst))
                    h_scr[bi, hd] = hst * jnp.exp(cg_last) + _mm(xdt.T, bm * jnp.exp(cg_last - cg_col))
            yc = jnp.concatenate(ys, axis=-1) + dsk_ref[...] * xv[bi, sl]
            yc = yc * _silu(z_ref[rows, :])
            outs = [_rms(yc[:, gi * gw:(gi + 1) * gw], ng[:, gi * gw:(gi + 1) * gw]) for gi in range(SSM_GROUPS)]
            y_ref[rows, :] = jnp.concatenate(outs, axis=-1).astype(y_ref.dtype)

    hfin_ref[...] = h_scr[...]


def _lane_row(vals, offset):
    depth, n = vals.shape
    return jnp.pad(vals.astype(F32), ((0, 0), (offset, LANES - offset - n))).reshape(depth, 1, LANES)


def _ssd(proj, conv_buf, h0, conv_w, conv_b, dt_bias, a_log, d_skip, norm_g, layer, b, t):
    zero_init = h0 is None
    c = math.gcd(t, SSM_CHUNK)
    nc = max(1, min(4, t // c))
    tb = c * nc
    nt = t // tb
    bb = math.gcd(b, 8) if nt == 1 else 1
    rb = bb * tb
    depth = conv_w.shape[0]
    row = lambda bi, i: bi * nt + i
    wblk = SSM_W
    st_dims = (SSM_HEADS, SSM_HEADDIM, SSM_STATE)
    in_specs = [pl.BlockSpec((rb, wblk), lambda bi, i: (row(bi, i), PK_SZ // wblk)),
                pl.BlockSpec((rb, wblk), lambda bi, i: (row(bi, i), PK_SX // wblk)),
                pl.BlockSpec((rb, wblk), lambda bi, i: (row(bi, i), PK_SBC // wblk)),
                pl.BlockSpec((rb, LANES), lambda bi, i: (row(bi, i), PK_SM // LANES)),
                pl.BlockSpec((None, SSM_CONV, wblk), lambda bi, i: (layer, 0, 0)),
                pl.BlockSpec((None, SSM_CONV, wblk), lambda bi, i: (layer, 0, 1)),
                pl.BlockSpec((None, 1, wblk), lambda bi, i: (layer, 0, 0)),
                pl.BlockSpec((None, 1, wblk), lambda bi, i: (layer, 0, 1)),
                pl.BlockSpec((None, 1, LANES), lambda bi, i: (layer, 0, 0)),
                pl.BlockSpec((None, 1, LANES), lambda bi, i: (layer, 0, 0)),
                pl.BlockSpec((None, 1, SSM_W), lambda bi, i: (layer, 0, 0)),
                pl.BlockSpec((None, 1, SSM_W), lambda bi, i: (layer, 0, 0))]
    args = [proj, proj, proj, proj, conv_w, conv_w, conv_b.reshape(depth, 1, -1), conv_b.reshape(depth, 1, -1),
            _lane_row(dt_bias, SM_DT), _lane_row(a_log, SM_DT),
            jnp.repeat(d_skip.astype(F32), SSM_HEADDIM, axis=-1).reshape(depth, 1, SSM_W),
            norm_g.reshape(depth, 1, SSM_W)]
    if not zero_init:
        in_specs += [pl.BlockSpec((None, bb, SSM_CONV - 1, wblk), lambda bi, i: (layer, bi, 0, 0)),
                     pl.BlockSpec((None, bb, SSM_CONV - 1, wblk), lambda bi, i: (layer, bi, 0, 1)),
                     pl.BlockSpec((None, bb) + st_dims, lambda bi, i: (layer, bi, 0, 0, 0))]
        args += [conv_buf, conv_buf, jnp.swapaxes(h0, -1, -2)]
    y, h_fin_t = pl.pallas_call(
        functools.partial(_ssd_kernel, c=c, nc=nc, bb=bb, zero_init=zero_init),
        grid=(b // bb, nt),
        in_specs=in_specs,
        out_specs=[pl.BlockSpec((rb, SSM_W), lambda bi, i: (row(bi, i), 0)),
                   pl.BlockSpec((bb,) + st_dims, lambda bi, i: (bi, 0, 0, 0))],
        out_shape=[jax.ShapeDtypeStruct((b * t, SSM_W), BF16), jax.ShapeDtypeStruct((b,) + st_dims, F32)],
        scratch_shapes=[pltpu.VMEM((bb, tb + SUBLANES, wblk), F32)] * 2 + [pltpu.VMEM((bb,) + st_dims, F32)],
        compiler_params=_cparams(("arbitrary", "arbitrary")),
        name="ssd",
    )(*args)
    return y, jnp.swapaxes(h_fin_t, -1, -2)


def _rope_tables(pos0, t):
    half = SWA_HD // 2
    inv = ROPE_THETA ** (-jnp.arange(half, dtype=F32) / half)
    ang = (pos0 + jnp.arange(t, dtype=jnp.int32)).astype(F32)[:, None] * inv[None, :]
    cos, sin = jnp.cos(ang), jnp.sin(ang)
    return jnp.concatenate([cos, cos], axis=-1), jnp.concatenate([-sin, sin], axis=-1)


def _rope(x, cosf, sinf):
    return x * cosf + pltpu.roll(x, SWA_HD // 2, axis=1) * sinf


def _swa_prompt_kernel(q_ref, k_ref, v_ref, cos_ref, sin_ref, k_prev, v_prev, o_ref, kr_ref, vo_ref,
                       qs, p_num, p_den, p_mx, *, t):
    del k_prev, v_prev
    qb = LANES
    cosf, sinf = cos_ref[...], sin_ref[...]
    qs[...] = _rope(q_ref[...], cosf, sinf) * (SWA_HD ** -0.5)
    kr_ref[...] = _rope(k_ref[...], cosf, sinf)
    vo_ref[...] = v_ref[...]

    m_i = lax.broadcasted_iota(jnp.int32, (qb, 2 * qb), 0)
    n_i = lax.broadcasted_iota(jnp.int32, (qb, 2 * qb), 1)
    mask_two = (n_i >= m_i) & (n_i <= m_i + qb)
    mask_one = (lax.broadcasted_iota(jnp.int32, (qb, qb), 1)
                <= lax.broadcasted_iota(jnp.int32, (qb, qb), 0))

    def block(br, q0, k0, nk, dil):
        if dil == 1:
            qi, ki = pl.ds(q0, qb), pl.ds(k0, nk)
        else:
            qi, ki = pl.ds(q0, qb, stride=dil), pl.ds(k0, nk, stride=dil)
        sc = _mm_nt(qs[qi, :], kr_ref[ki, :])
        sc = jnp.where(mask_one if nk == qb else mask_two, sc, NEG_BIG)
        mx = jnp.max(sc, axis=-1, keepdims=True)
        p = jnp.exp(sc - mx)
        p_num[br, qi, :] = _mm(p, v_ref[ki, :])
        p_den[br, qi, :] = jnp.broadcast_to(jnp.sum(p, axis=-1, keepdims=True), (qb, LANES))
        p_mx[br, qi, :] = jnp.broadcast_to(mx, (qb, LANES))

    for br, (window, dil) in enumerate(DILATIONS):
        assert window == qb * dil and t % (qb * dil) == 0
        nblk = t // (qb * dil)

        def residue(r, carry, br=br, dil=dil, nblk=nblk):
            block(br, r, r, qb, dil)
            if nblk > 1:
                def later(j, cc):
                    block(br, r + dil * qb * j, r + dil * qb * (j - 1), 2 * qb, dil)
                    return cc
                lax.fori_loop(1, nblk, later, 0, unroll=5 if (nblk - 1) % 5 == 0 else 3)
            return carry

        if dil == 1:
            residue(0, 0)
        else:
            lax.fori_loop(0, dil, residue, 0, unroll=8 if nblk == 1 else 2)

    rows = 2 * qb

    def merge(i, carry):
        sl = pl.ds(pl.multiple_of(i * rows, rows), rows)
        mxs = [p_mx[br, sl, :] for br in range(len(DILATIONS))]
        mx = functools.reduce(jnp.maximum, mxs)
        wts = [jnp.exp(m - mx) for m in mxs]
        num = sum(p_num[br, sl, :] * wts[br] for br in range(len(DILATIONS)))
        den = sum(p_den[br, sl, :] * wts[br] for br in range(len(DILATIONS)))
        o_ref[sl, :] = (num / den).astype(o_ref.dtype)
        return carry

    lax.fori_loop(0, t // rows, merge, 0)


def _swa_prompt(proj, layer, k_all, v_all, b, t):
    cosf, sinf = _rope_tables(0, t)
    hq, hk, hv = PK_AQ // LANES, PK_AK // LANES, PK_AV // LANES
    kv_spec = pl.BlockSpec((None, None, None, t, SWA_HD), lambda bi, h: (layer, bi, h, 0, 0))
    hbm = pl.BlockSpec(memory_space=pl.ANY)
    return pl.pallas_call(
        functools.partial(_swa_prompt_kernel, t=t),
        grid=(b, SWA_HEADS),
        in_specs=[pl.BlockSpec((t, LANES), lambda bi, h: (bi, hq + h)),
                  pl.BlockSpec((t, LANES), lambda bi, h: (bi, hk + h)),
                  pl.BlockSpec((t, LANES), lambda bi, h: (bi, hv + h)),
                  pl.BlockSpec((t, LANES), lambda bi, h: (0, 0)),
                  pl.BlockSpec((t, LANES), lambda bi, h: (0, 0)),
                  hbm, hbm],
        out_specs=[pl.BlockSpec((t, LANES), lambda bi, h: (bi, h)), kv_spec, kv_spec],
        out_shape=[jax.ShapeDtypeStruct((b * t, SWA_W), BF16),
                   jax.ShapeDtypeStruct(k_all.shape, F32), jax.ShapeDtypeStruct(v_all.shape, F32)],
        input_output_aliases={5: 1, 6: 2},
        scratch_shapes=[pltpu.VMEM((t, LANES), F32)] + [pltpu.VMEM((len(DILATIONS), t, LANES), F32)] * 3,
        compiler_params=_cparams(("arbitrary", "arbitrary")),
        name="swa_prompt",
    )(proj, proj, proj, cosf, sinf, k_all, v_all)


def _swa_sample_kernel(q_ref, k_ref, v_ref, cos_ref, sin_ref, kf, vf, kn, vn, o_ref, kr_ref, *, t, past):
    (w1, d1), (w2, d2), (w3, d3) = DILATIONS
    n3 = kf.shape[1] * kf.shape[2]
    n2 = kn.shape[1] * kn.shape[2]
    n_l = kn.shape[1] * d3
    tail0 = n3 - kn.shape[1] * t
    assert d3 == 2 * t and w3 == past and n_l == w2 and w1 <= n_l and tail0 % LANES == 0
    assert all(d & (d - 1) == 0 for d in (t, d1, d2, d3))
    cosf, sinf = cos_ref[...], sin_ref[...]

    def masks(n, abs_of_col, branches):
        m_i = lax.broadcasted_iota(jnp.int32, (t, n), 0)
        dist = past + m_i - abs_of_col(lax.broadcasted_iota(jnp.int32, (t, n), 1))
        return [(dist >= 0) & (dist <= w) & ((dist & (d - 1)) == 0) for w, d in branches]

    shift = t.bit_length() - 1
    near_br, far_br = DILATIONS[:2], DILATIONS[2:]
    (mask3f,) = masks(n3, lambda c: d3 * (c >> shift) + (c & (t - 1)), far_br)
    mask1t, mask2t = masks(n3 - tail0, lambda c: d3 * ((c + tail0) >> shift) + (c & (t - 1)), near_br)
    mask1c, mask2c = masks(n2, lambda c: past - n_l + d3 * (c >> shift) + t + (c & (t - 1)), near_br)
    mask1n, mask2n, mask3n = masks(t, lambda c: past + c, DILATIONS)

    for h in range(SWA_HEADS):
        hs = slice(h * SWA_HD, (h + 1) * SWA_HD)
        qh = _rope(q_ref[:, hs], cosf, sinf) * (SWA_HD ** -0.5)
        k_new = _rope(k_ref[:, hs], cosf, sinf)
        kr_ref[:, hs] = k_new
        v_new = v_ref[:, hs]
        k3 = kf[h].reshape(n3, SWA_HD)
        v3 = vf[h].reshape(n3, SWA_HD)
        k2 = kn[h].reshape(n2, SWA_HD)
        v2 = vn[h].reshape(n2, SWA_HD)
        s3 = _mm_nt(qh, k3)
        s_t = s3[:, tail0:]
        s_c = _mm_nt(qh, k2)
        s_n = _mm_nt(qh, k_new)

        def branch(parts):
            mx = None
            for sc, mask in parts:
                cur = jnp.max(jnp.where(mask, sc, NEG_BIG), axis=-1, keepdims=True)
                mx = cur if mx is None else jnp.maximum(mx, cur)
            ps = [jnp.where(mask, jnp.exp(jnp.where(mask, sc, NEG_BIG) - mx), 0.0) for sc, mask in parts]
            den = sum(jnp.sum(p, axis=-1, keepdims=True) for p in ps)
            return mx, ps, den

        mx1, (p1t, p1c, p1n), den1 = branch([(s_t, mask1t), (s_c, mask1c), (s_n, mask1n)])
        mx2, (p2t, p2c, p2n), den2 = branch([(s_t, mask2t), (s_c, mask2c), (s_n, mask2n)])
        mx3, (p3f, p3n), den3 = branch([(s3, mask3f), (s_n, mask3n)])
        mx = jnp.maximum(jnp.maximum(mx1, mx2), mx3)
        wt1, wt2, wt3 = jnp.exp(mx1 - mx), jnp.exp(mx2 - mx), jnp.exp(mx3 - mx)
        num = (_mm(p3f * wt3, v3) + _mm(p1t * wt1 + p2t * wt2, v3[tail0:])
               + _mm(p1c * wt1 + p2c * wt2, v2) + _mm(p1n * wt1 + p2n * wt2 + p3n * wt3, v_new))
        den = den1 * wt1 + den2 * wt2 + den3 * wt3
        o_ref[:, hs] = (num / den).astype(o_ref.dtype)


def _swa_sample(proj, cache_k, cache_v, layer, b, t, past):
    depth, _, l_cache, nh, hd = cache_k.shape
    assert l_cache == past and nh * hd == SWA_W
    (_, _), (w2, _), (_, d3) = DILATIONS
    cosf, sinf = _rope_tables(past, t)
    assert l_cache % d3 == 0 and l_cache % w2 == 0 and t % SUBLANES == 0
    ckh = jnp.swapaxes(cache_k, 2, 3)
    cvh = jnp.swapaxes(cache_v, 2, 3)
    assert d3 == 2 * t
    far = pl.BlockSpec((None, None, nh, l_cache // d3, t, hd), lambda bi: (layer, bi, 0, 0, 0, 0))
    near = pl.BlockSpec((None, None, nh, w2 // d3, t, hd), lambda bi: (layer, bi, 0, l_cache // w2 - 1, 1, 0))
    split = lambda c: c.reshape(depth, b, nh, l_cache // d3, d3, hd)
    return pl.pallas_call(
        functools.partial(_swa_sample_kernel, t=t, past=past),
        grid=(b,),
        in_specs=[pl.BlockSpec((t, SWA_W), lambda bi: (bi, PK_AQ // SWA_W)),
                  pl.BlockSpec((t, SWA_W), lambda bi: (bi, PK_AK // SWA_W)),
                  pl.BlockSpec((t, SWA_W), lambda bi: (bi, PK_AV // SWA_W)),
                  pl.BlockSpec((t, LANES), lambda bi: (0, 0)),
                  pl.BlockSpec((t, LANES), lambda bi: (0, 0)),
                  far, far, near, near],
        out_specs=[pl.BlockSpec((t, SWA_W), lambda bi: (bi, 0)),
                   pl.BlockSpec((t, SWA_W), lambda bi: (bi, 0))],
        out_shape=[jax.ShapeDtypeStruct((b * t, SWA_W), BF16),
                   jax.ShapeDtypeStruct((b * t, SWA_W), F32)],
        compiler_params=_cparams(("arbitrary",)),
        name="swa_sample",
    )(proj, proj, proj, cosf, sinf, split(ckh), split(cvh), split(ckh), split(cvh))


def _outproj_kernel(oa_ref, ob_ref, oc_ref, w_ref, x_ref, gate_ref, gpost_ref, gpre_ref, sh_ref, sc_ref,
                    xo_ref, h2_ref):
    mix = (jnp.dot(oa_ref[...], w_ref[0:GDN_W, :], preferred_element_type=F32)
           + jnp.dot(ob_ref[...], w_ref[GDN_W:GDN_W + SSM_W, :], preferred_element_type=F32)
           + jnp.dot(oc_ref[...], w_ref[GDN_W + SSM_W:, :], preferred_element_type=F32))
    y = _rms(mix, gpost_ref[...])
    x = x_ref[...] + gate_ref[...] * y.reshape(x_ref.shape)
    xo_ref[...] = x
    h2 = _rms(x, gpre_ref[...]) * (1.0 + sc_ref[...]) + sh_ref[...]
    h2_ref[...] = h2.reshape(h2_ref.shape).astype(h2_ref.dtype)


def _outproj(oa, ob, oc, w, x, mod, gpost, gpre, layer, bb, tt):
    b, t, d = x.shape
    nt = t // tt
    rows = bb * tt
    rmap = lambda bi, ti: (bi * nt + ti, 0)
    mspec = lambda k: pl.BlockSpec((None, bb, 1, d), lambda bi, ti: (layer, bi, 0, k))
    gspec = pl.BlockSpec((None, 1, d), lambda bi, ti: (layer, 0, 0))
    return pl.pallas_call(
        _outproj_kernel,
        grid=(b // bb, nt),
        in_specs=[pl.BlockSpec((rows, GDN_W), rmap), pl.BlockSpec((rows, SSM_W), rmap),
                  pl.BlockSpec((rows, SWA_W), rmap),
                  pl.BlockSpec((None, d, d), lambda bi, ti: (layer, 0, 0)),
                  pl.BlockSpec((bb, tt, d), lambda bi, ti: (bi, ti, 0)),
                  mspec(2), gspec, gspec, mspec(3), mspec(4)],
        out_specs=[pl.BlockSpec((bb, tt, d), lambda bi, ti: (bi, ti, 0)),
                   pl.BlockSpec((rows, d), rmap)],
        out_shape=[jax.ShapeDtypeStruct((b, t, d), F32), jax.ShapeDtypeStruct((b * t, d), BF16)],
        compiler_params=_cparams(("arbitrary", "arbitrary")),
        name="outproj",
    )(oa, ob, oc, w, x, mod, gpost, gpre, mod, mod)


def _ffn_up_kernel(*refs, tt, zero_init):
    if zero_init:
        (h_ref, wg_ref, wu_ref, cwg_ref, cwu_ref, cbg_ref, cbu_ref, act_ref, nsg_ref, nsu_ref, csg, csu) = refs
        stg_ref = stu_ref = None
    else:
        (h_ref, wg_ref, wu_ref, cwg_ref, cwu_ref, cbg_ref, cbu_ref, stg_ref, stu_ref,
         act_ref, nsg_ref, nsu_ref, csg, csu) = refs
    first = pl.program_id(2) == 0
    lo = SUBLANES - (FFN_CONV - 1)
    bb, _, tn = csg.shape

    for st_ref, cs in ((stg_ref, csg), (stu_ref, csu)):
        @pl.when(first)
        def _(st_ref=st_ref, cs=cs):
            if st_ref is None:
                cs[:, lo:SUBLANES, :] = jnp.zeros((bb, FFN_CONV - 1, tn), F32)
            else:
                cs[:, lo:SUBLANES, :] = st_ref[...]

        @pl.when(jnp.logical_not(first))
        def _(cs=cs):
            cs[:, lo:SUBLANES, :] = cs[:, tt + lo:tt + SUBLANES, :]

    h = h_ref[...]
    sub = 2 * LANES

    def half(w_ref, cw_ref, cb_ref, ns_ref, cs, cols):
        up = jnp.dot(h, w_ref[:, cols], preferred_element_type=F32)
        cs[:, SUBLANES:SUBLANES + tt, cols] = up.reshape(bb, tt, up.shape[-1])
        y = cs[:, lo:lo + tt, cols] * cw_ref[0:1, cols]
        for j in range(1, FFN_CONV):
            y = y + cs[:, lo + j:lo + j + tt, cols] * cw_ref[j:j + 1, cols]
        ns_ref[:, :, cols] = cs[:, tt + lo:tt + SUBLANES, cols]
        return y + cb_ref[:, cols]

    for c0 in range(0, tn, sub):
        cols = slice(c0, min(c0 + sub, tn))
        yg = half(wg_ref, cwg_ref, cbg_ref, nsg_ref, csg, cols)
        yu = half(wu_ref, cwu_ref, cbu_ref, nsu_ref, csu, cols)
        act = _silu(yg) * yu
        act_ref[:, cols] = act.reshape(bb * tt, act.shape[-1]).astype(act_ref.dtype)


def _ffn_up(h2, state, w_up, conv_w, conv_b, layer, b, t, bb, tt):
    zero_init = state is None
    d = h2.shape[1]
    tn = 1408 if (bb == 1 and tt <= 1024) else 512
    nh = D_FF // tn
    nt = t // tt
    rows = bb * tt
    depth = conv_w.shape[0]
    cb = conv_b.reshape(depth, 1, -1)
    in_specs = [pl.BlockSpec((rows, d), lambda bi, j, ti: (bi * nt + ti, 0)),
                pl.BlockSpec((None, d, tn), lambda bi, j, ti: (layer, 0, j)),
                pl.BlockSpec((None, d, tn), lambda bi, j, ti: (layer, 0, nh + j)),
                pl.BlockSpec((None, FFN_CONV, tn), lambda bi, j, ti: (layer, 0, j)),
                pl.BlockSpec((None, FFN_CONV, tn), lambda bi, j, ti: (layer, 0, nh + j)),
                pl.BlockSpec((None, 1, tn), lambda bi, j, ti: (layer, 0, j)),
                pl.BlockSpec((None, 1, tn), lambda bi, j, ti: (layer, 0, nh + j))]
    args = [h2, w_up, w_up, conv_w, conv_w, cb, cb]
    if not zero_init:
        in_specs += [pl.BlockSpec((None, bb, FFN_CONV - 1, tn), lambda bi, j, ti: (layer, bi, 0, j)),
                     pl.BlockSpec((None, bb, FFN_CONV - 1, tn), lambda bi, j, ti: (layer, bi, 0, nh + j))]
        args += [state, state]
    ns_spec = pl.BlockSpec((bb, FFN_CONV - 1, tn), lambda bi, j, ti: (bi, 0, j))
    act, nsg, nsu = pl.pallas_call(
        functools.partial(_ffn_up_kernel, tt=tt, zero_init=zero_init),
        grid=(b // bb, nh, nt),
        in_specs=in_specs,
        out_specs=[pl.BlockSpec((rows, tn), lambda bi, j, ti: (bi * nt + ti, j)), ns_spec, ns_spec],
        out_shape=[jax.ShapeDtypeStruct((b * t, D_FF), BF16),
                   jax.ShapeDtypeStruct((b, FFN_CONV - 1, D_FF), F32),
                   jax.ShapeDtypeStruct((b, FFN_CONV - 1, D_FF), F32)],
        scratch_shapes=[pltpu.VMEM((bb, tt + SUBLANES, tn), F32)] * 2,
        compiler_params=_cparams(("arbitrary", "arbitrary", "arbitrary")),
        name="ffn_up",
    )(*args)
    return act, jnp.concatenate([nsg, nsu], axis=-1)


def _ffn_down_kernel(a_ref, w_ref, x_ref, gate_ref, g_ref, o_ref):
    k = pl.program_id(2)
    part = lambda: jnp.dot(a_ref[...], w_ref[...], preferred_element_type=F32).reshape(o_ref.shape)

    @pl.when(k == 0)
    def _():
        o_ref[...] = part()

    @pl.when(k > 0)
    def _():
        o_ref[...] += part()

    @pl.when(k == pl.num_programs(2) - 1)
    def _():
        o_ref[...] = x_ref[...] + gate_ref[...] * _rms(o_ref[...], g_ref[...])


def _ffn_down(act, w_down, x, mod, g, layer, bb, tt):
    b, t, d = x.shape
    nt = t // tt
    rows = bb * tt
    tk = 512
    return pl.pallas_call(
        _ffn_down_kernel,
        grid=(b // bb, nt, D_FF // tk),
        in_specs=[pl.BlockSpec((rows, tk), lambda bi, ti, k: (bi * nt + ti, k)),
                  pl.BlockSpec((None, tk, d), lambda bi, ti, k: (layer, k, 0)),
                  pl.BlockSpec((bb, tt, d), lambda bi, ti, k: (bi, ti, 0)),
                  pl.BlockSpec((None, bb, 1, d), lambda bi, ti, k: (layer, bi, 0, 5)),
                  pl.BlockSpec((None, 1, d), lambda bi, ti, k: (layer, 0, 0))],
        out_specs=pl.BlockSpec((bb, tt, d), lambda bi, ti, k: (bi, ti, 0)),
        out_shape=jax.ShapeDtypeStruct((b, t, d), F32),
        compiler_params=_cparams(("arbitrary", "arbitrary", "arbitrary")),
        name="ffn_down",
    )(act, w_down, x, mod, g)


def _pack_w_in(w_in):
    depth, d, _ = w_in.shape
    o_gb = 4 * GDN_W
    o_sz = o_gb + 2 * GDN_HEADS
    o_dt = o_sz + SSM_W + SSM_CONV_CH
    o_aq = o_dt + SSM_HEADS
    pad = jnp.zeros((depth, d, LANES - 2 * GDN_HEADS - SSM_HEADS), w_in.dtype)
    packed = jnp.concatenate([w_in[..., :o_gb], w_in[..., o_sz:o_dt], w_in[..., o_aq:],
                              w_in[..., o_gb:o_sz], w_in[..., o_dt:o_aq], pad], axis=-1)
    assert packed.shape[-1] == PK_COLS
    return packed.astype(BF16)


def _layer(x, mod, layer, states, kv_all, wts, past, bb, tt):
    b, t, d = x.shape
    if states is None:
        gdn_conv = gdn_s = ssm_conv = ssm_h = cache_k = cache_v = ffn_conv = None
    else:
        gdn_conv, gdn_s, ssm_conv, ssm_h, cache_k, cache_v, ffn_conv = states
    tt2 = 2 * tt if (bb == 1 and t % (2 * tt) == 0) else tt
    proj = _inproj(x, mod, wts["g_pre_mix"], wts["w_in"], layer, bb, tt2)
    gdn_w = (wts["gdn_conv_w"], wts["gdn_a_log"], wts["gdn_dt_bias"], wts["gdn_norm_g"])
    if states is None:
        o_a, new_gdn_s = _gdn_seq(proj, *gdn_w, layer, b, t)
    else:
        o_a, new_gdn_s = _gdn_step(proj, gdn_conv, gdn_s, *gdn_w, layer, b, t)
    o_b, new_ssm_h = _ssd(proj, ssm_conv, ssm_h, wts["ssm_conv_w"], wts["ssm_conv_b"], wts["ssm_dt_bias"],
                          wts["ssm_a_log"], wts["ssm_d"], wts["ssm_norm_g"], layer, b, t)
    if states is None:
        o_c, *kv_all = _swa_prompt(proj, layer, *kv_all, b, t)
    else:
        o_c, k_rot = _swa_sample(proj, cache_k, cache_v, layer, b, t, past)
    x_mid, h2 = _outproj(o_a, o_b, o_c, wts["w_out"], x, mod, wts["g_post_mix"], wts["g_pre_ffn"], layer, bb, tt)
    act, new_ffn_conv = _ffn_up(h2, ffn_conv, wts["w_up"], wts["ffn_conv_w"], wts["ffn_conv_b"], layer, b, t, bb, tt2)
    x_out = _ffn_down(act, wts["w_down"], x_mid, mod, wts["g_post_ffn"], layer, bb, tt2)

    proj3 = proj.reshape(b, t, PK_COLS)
    new_gdn_conv = proj3[:, t - (GDN_CONV - 1):, PK_GQ:PK_GQ + 3 * GDN_W]
    new_ssm_conv = proj3[:, t - (SSM_CONV - 1):, PK_SX:PK_SX + SSM_CONV_CH]
    outs = [new_gdn_conv, new_gdn_s, new_ssm_conv, new_ssm_h, new_ffn_conv]
    if states is not None:
        outs += [k_rot.reshape(b, t, SWA_HEADS, SWA_HD), proj3[:, :, PK_AV:PK_AV + SWA_W].reshape(b, t, SWA_HEADS, SWA_HD)]
    return x_out, outs, kv_all


def kernel(x_prompt, x_sample, c_prompt, c_sample, state_gdn_conv, state_gdn, state_ssm_conv, state_ssm, cache_k, cache_v, state_ffn_conv, w_ada, b_ada, g_pre_mix, g_post_mix, g_pre_ffn, g_post_ffn, w_in, gdn_conv_w, gdn_a_log, gdn_dt_bias, gdn_norm_g, ssm_conv_w, ssm_conv_b, ssm_dt_bias, ssm_a_log, ssm_d, ssm_norm_g, w_out, w_up, ffn_conv_w, ffn_conv_b, w_down):
    depth = w_ada.shape[0]
    bp, tp, d = x_prompt.shape
    bs, ts, _ = x_sample.shape
    past = cache_k.shape[2]
    assert tp >= GDN_CONV and ts >= GDN_CONV and ts % SUBLANES == 0

    vec = lambda a: a.reshape(depth, 1, a.shape[-1])
    wts = dict(
        g_pre_mix=vec(g_pre_mix), g_post_mix=vec(g_post_mix), g_pre_ffn=vec(g_pre_ffn), g_post_ffn=vec(g_post_ffn),
        w_in=_pack_w_in(w_in), w_out=w_out.astype(BF16), w_up=w_up.astype(BF16), w_down=w_down.astype(BF16),
        gdn_conv_w=gdn_conv_w, gdn_a_log=gdn_a_log, gdn_dt_bias=gdn_dt_bias, gdn_norm_g=gdn_norm_g,
        ssm_conv_w=ssm_conv_w, ssm_conv_b=ssm_conv_b, ssm_dt_bias=ssm_dt_bias, ssm_a_log=ssm_a_log, ssm_d=ssm_d,
        ssm_norm_g=ssm_norm_g, ffn_conv_w=ffn_conv_w, ffn_conv_b=ffn_conv_b)

    mod = _ada(jnp.concatenate([c_prompt, c_sample], axis=0), w_ada, b_ada)
    mod_p = mod[:, :bp].reshape(depth, bp, 1, 6 * d)
    mod_s = mod[:, bp:].reshape(depth, bs, 1, 6 * d)
    s_states = (state_gdn_conv, state_gdn, state_ssm_conv, state_ssm, cache_k, cache_v, state_ffn_conv)

    tt_p = math.gcd(tp, 512)
    bb_s = math.gcd(bs, 512 // ts)
    assert tp <= W_MAX and ts <= W_MAX
    y_p, y_s = x_prompt, x_sample
    kv_p = [jnp.zeros((depth, bp, SWA_HEADS, tp, SWA_HD), F32) for _ in range(2)]
    p_rows, s_rows = [], []
    for layer in range(depth):
        y_p, outs_p, kv_p = _layer(y_p, mod_p, layer, None, kv_p, wts, 0, 1, tt_p)
        y_s, outs_s, _ = _layer(y_s, mod_s, layer, s_states, None, wts, past, bb_s, ts)
        p_rows.append(outs_p)
        s_rows.append(outs_s)
    p_gdn_conv, p_gdn, p_ssm_conv, p_ssm, p_ffn_conv = [jnp.stack(a) for a in zip(*p_rows)]
    s_gdn_conv, s_gdn, s_ssm_conv, s_ssm, s_ffn_conv, s_k, s_v = [jnp.stack(a) for a in zip(*s_rows)]
    p_k, p_v = [jnp.swapaxes(a, 2, 3) for a in kv_p]
    return (y_p, y_s, p_gdn_conv, p_gdn, p_ssm_conv, p_ssm, p_k, p_v, p_ffn_conv,
            s_gdn_conv, s_gdn, s_ssm_conv, s_ssm, s_k, s_v, s_ffn_conv)
```

```python
import functools
import math

import jax
import jax.numpy as jnp
import numpy as np
from jax import lax
from jax.experimental import pallas as pl
from jax.experimental.pallas import tpu as pltpu

F32 = jnp.float32
BF16 = jnp.bfloat16

D_MODEL = 2048
MIX_UNIT = D_MODEL // 8
GDN_W = 3 * MIX_UNIT
SSM_W = 2 * MIX_UNIT
SWA_W = 3 * MIX_UNIT
GDN_DK = 128
GDN_DV = 128
GDN_HEADS = GDN_W // GDN_DV
GDN_CONV = 4
GDN_CHUNK = 64
SSM_HEADDIM = 64
SSM_HEADS = SSM_W // SSM_HEADDIM
SSM_GROUPS = 2
SSM_STATE = 128
SSM_CONV = 4
SSM_CHUNK = 64
SSM_CONV_CH = SSM_W + 2 * SSM_GROUPS * SSM_STATE
SWA_HD = 128
SWA_HEADS = SWA_W // SWA_HD
DILATIONS = ((128, 1), (512, 4), (2048, 16))
W_MAX = 2048
ROPE_THETA = 10000.0
D_FF = 11 * D_MODEL // 4
FFN_CONV = 3
NORM_EPS = 1e-6

LANES = 128
SUBLANES = 8
VMEM_LIMIT = 56 * 1024 * 1024

PK_GQ = 0
PK_GK = PK_GQ + GDN_W
PK_GV = PK_GK + GDN_W
PK_GZ = PK_GV + GDN_W
PK_SZ = PK_GZ + GDN_W
PK_SX = PK_SZ + SSM_W
PK_SBC = PK_SX + SSM_W
PK_AQ = PK_SX + SSM_CONV_CH
PK_AK = PK_AQ + SWA_W
PK_AV = PK_AK + SWA_W
PK_SM = PK_AV + SWA_W
PK_COLS = PK_SM + LANES
SM_BETA = 0
SM_A = GDN_HEADS
SM_DT = 2 * GDN_HEADS
NEG_BIG = -1e30
GDN_SEQ_HEADS = 3
GDN_SEQ_CHUNKS = 8


def _cparams(sem):
    return pltpu.CompilerParams(dimension_semantics=sem, vmem_limit_bytes=VMEM_LIMIT)


def _sigmoid(x):
    return 1.0 / (1.0 + jnp.exp(-x))


def _silu(x):
    return x * _sigmoid(x)


def _softplus(x):
    return jnp.maximum(x, 0.0) + jnp.log1p(jnp.exp(-jnp.abs(x)))


def _mm(a, b):
    return jnp.dot(a.astype(BF16), b.astype(BF16), preferred_element_type=F32)


def _mm_nt(a, b):
    return lax.dot_general(a.astype(BF16), b.astype(BF16), (((1,), (1,)), ((), ())), preferred_element_type=F32)


def _mm_hi(a, b):
    return jnp.dot(a, b, preferred_element_type=F32, precision=lax.Precision.HIGHEST)


def _rms(x, g):
    return x * lax.rsqrt(jnp.mean(x * x, axis=-1, keepdims=True) + NORM_EPS) * g


def _ada_kernel(c_ref, w_ref, b_ref, o_ref):
    a = _silu(c_ref[...]).astype(BF16)
    o_ref[...] = jnp.dot(a, w_ref[...].astype(BF16), preferred_element_type=F32) + b_ref[...]


def _ada(c_all, w_ada, b_ada):
    depth, d, n = w_ada.shape
    r = c_all.shape[0]
    tn = 1024
    return pl.pallas_call(
        _ada_kernel,
        grid=(depth, n // tn),
        in_specs=[pl.BlockSpec((r, d), lambda l, j: (0, 0)),
                  pl.BlockSpec((None, d, tn), lambda l, j: (l, 0, j)),
                  pl.BlockSpec((None, 1, tn), lambda l, j: (l, 0, j))],
        out_specs=pl.BlockSpec((None, r, tn), lambda l, j: (l, 0, j)),
        out_shape=jax.ShapeDtypeStruct((depth, r, n), F32),
        compiler_params=_cparams(("arbitrary", "arbitrary")),
        name="ada",
    )(c_all, w_ada, b_ada.reshape(depth, 1, n))


def _inproj_kernel(x_ref, sh_ref, sc_ref, g_ref, w_ref, o_ref, h_scr):
    @pl.when(pl.program_id(2) == 0)
    def _():
        h = _rms(x_ref[...], g_ref[...]) * (1.0 + sc_ref[...]) + sh_ref[...]
        h_scr[...] = h.reshape(h_scr.shape).astype(BF16)

    o_ref[...] = jnp.dot(h_scr[...], w_ref[...], preferred_element_type=F32)


def _inproj(x, mod, g, w, layer, bb, tt):
    b, t, d = x.shape
    n = w.shape[-1]
    tn = 1408
    nt = t // tt
    return pl.pallas_call(
        _inproj_kernel,
        grid=(b // bb, nt, n // tn),
        in_specs=[pl.BlockSpec((bb, tt, d), lambda bi, ti, j: (bi, ti, 0)),
                  pl.BlockSpec((None, bb, 1, d), lambda bi, ti, j: (layer, bi, 0, 0)),
                  pl.BlockSpec((None, bb, 1, d), lambda bi, ti, j: (layer, bi, 0, 1)),
                  pl.BlockSpec((None, 1, d), lambda bi, ti, j: (layer, 0, 0)),
                  pl.BlockSpec((None, d, tn), lambda bi, ti, j: (layer, 0, j))],
        out_specs=pl.BlockSpec((bb * tt, tn), lambda bi, ti, j: (bi * nt + ti, j)),
        out_shape=jax.ShapeDtypeStruct((b * t, n), F32),
        scratch_shapes=[pltpu.VMEM((bb * tt, d), BF16)],
        compiler_params=_cparams(("arbitrary", "arbitrary", "arbitrary")),
        name="inproj",
    )(x, mod, mod, g, w)


def _conv_from_scratch(xs, w_ref, width, t):
    lo = SUBLANES - (width - 1)
    y = xs[lo:lo + t, :] * w_ref[0:1, :]
    for j in range(1, width):
        y = y + xs[lo + j:lo + j + t, :] * w_ref[j:j + 1, :]
    return y


def _conv_stage(xs, x_new, hist_ref, first, width, t):
    lo = SUBLANES - (width - 1)

    @pl.when(first)
    def _():
        if hist_ref is None:
            xs[lo:SUBLANES, :] = jnp.zeros((width - 1, xs.shape[1]), F32)
        else:
            xs[lo:SUBLANES, :] = hist_ref[...]

    @pl.when(jnp.logical_not(first))
    def _():
        xs[lo:SUBLANES, :] = xs[t + lo:t + SUBLANES, :]

    xs[SUBLANES:SUBLANES + t, :] = x_new


def _tri_masks(c):
    row = lax.broadcasted_iota(jnp.int32, (c, c), 0)
    col = lax.broadcasted_iota(jnp.int32, (c, c), 1)
    return row == col, row >= col, row > col, row <= col


def _bdot(a, b, ca, cb):
    return lax.dot_general(a, b, (((ca,), (cb,)), ((0,), (0,))), preferred_element_type=F32)


def _bmm(a, b):
    return _bdot(a.astype(BF16), b.astype(BF16), 2, 1)


def _bmm_nt(a, b):
    return _bdot(a.astype(BF16), b.astype(BF16), 2, 2)


def _bmm_tn(a, b):
    return _bdot(a.astype(BF16), b.astype(BF16), 1, 1)


def _split_bf16(a):
    hi = a.astype(BF16)
    return hi, (a - hi.astype(F32)).astype(BF16)


def _bmm3(a, b):
    ah, al = _split_bf16(a)
    bh, bl = _split_bf16(b)
    return _bdot(ah, bh, 2, 1) + _bdot(ah, bl, 2, 1) + _bdot(al, bh, 2, 1)


def _unit_lower_inverse(a, eye_f, c):
    x = eye_f - a
    p = _bmm3(a, a)
    n = 2
    while True:
        x = x + _bmm3(x, p)
        n *= 2
        if n >= c:
            return x
        p = _bmm3(p, p)


def _gdn_prep(q, k, v, beta, g, c):
    eye, causal, strict, upper = _tri_masks(c)
    g_row = jnp.sum(jnp.where(eye, g, 0.0), axis=1, keepdims=True)
    cg_col = jnp.sum(jnp.where(causal, g_row, 0.0), axis=2, keepdims=True)
    cg_row = jnp.sum(jnp.where(upper, g, 0.0), axis=1, keepdims=True)
    gam = jnp.where(causal, jnp.exp(jnp.where(causal, cg_col - cg_row, 0.0)), 0.0)
    kb = k * beta
    amat = jnp.where(strict, _bmm_nt(kb, k) * gam, 0.0)
    tinv = _unit_lower_inverse(amat, eye.astype(F32), c)
    ecg = jnp.exp(cg_col)
    u = _bmm3(tinv, v * beta)
    w = _bmm3(tinv, kb * ecg)
    qk = _bmm_nt(q, k) * gam
    cg_last = cg_col[:, c - 1:c, :]
    kd = k * jnp.exp(cg_last - cg_col)
    return u, w, qk, q * ecg, kd, jnp.exp(cg_last)


def _gdn_gates(b_raw, a_raw, a_log, dt_bias):
    a_neg = -jnp.exp(jnp.full((1, 1), a_log, F32))
    return _sigmoid(b_raw), a_neg * _softplus(a_raw + dt_bias)


def _l2norm(x):
    return x * lax.rsqrt(jnp.sum(x * x, axis=-1, keepdims=True) + NORM_EPS)


def _gdn_seq_kernel(alog_ref, dtb_ref, q_ref, k_ref, v_ref, z_ref, sm_ref, wq_ref, wk_ref, wv_ref, ng_ref,
                    o_ref, sfin_ref, xq, xk, xv, s_scr, *, c, nc, hp):
    hg = pl.program_id(1)
    tb = c * nc
    first = pl.program_id(2) == 0

    @pl.when(first)
    def _():
        s_scr[...] = jnp.zeros_like(s_scr)

    _conv_stage(xq, q_ref[...], None, first, GDN_CONV, tb)
    _conv_stage(xk, k_ref[...], None, first, GDN_CONV, tb)
    _conv_stage(xv, v_ref[...], None, first, GDN_CONV, tb)
    qa = _silu(_conv_from_scratch(xq, wq_ref, GDN_CONV, tb))
    ka = _silu(_conv_from_scratch(xk, wk_ref, GDN_CONV, tb))
    va = _silu(_conv_from_scratch(xv, wv_ref, GDN_CONV, tb))

    sm = sm_ref[...]
    lane = lax.broadcasted_iota(jnp.int32, sm.shape, 1)
    chunks = lambda a: a.reshape(nc, c, a.shape[-1])
    parts = []
    for j in range(hp):
        h = hg * hp + j
        hs = slice(j * GDN_DK, (j + 1) * GDN_DK)
        b_raw = jnp.sum(jnp.where(lane == SM_BETA + h, sm, 0.0), axis=1, keepdims=True)
        a_raw = jnp.sum(jnp.where(lane == SM_A + h, sm, 0.0), axis=1, keepdims=True)
        beta, g = _gdn_gates(b_raw, a_raw, alog_ref[h], dtb_ref[h])
        parts.append([chunks(a) for a in (_l2norm(qa[:, hs]) * (GDN_DK ** -0.5), _l2norm(ka[:, hs]), va[:, hs],
                                          beta, g)])
    u, w, qk, qg, kd, gl = _gdn_prep(*[jnp.concatenate(a, axis=0) for a in zip(*parts)], c)

    ng = ng_ref[...]
    states = [s_scr[j] for j in range(hp)]
    for ci in range(nc):
        sl = slice(ci * c, (ci + 1) * c)
        for j in range(hp):
            p = j * nc + ci
            hs = slice(j * GDN_DV, (j + 1) * GDN_DV)
            ws = _mm(jnp.concatenate([w[p], qg[p]], axis=0), states[j])
            vnew = u[p] - ws[:c]
            r = _mm(jnp.concatenate([qk[p], kd[p].T], axis=0), vnew)
            states[j] = states[j] * gl[p] + r[c:]
            o = _rms(ws[c:] + r[:c], ng) * _silu(z_ref[sl, hs])
            o_ref[sl, hs] = o.astype(o_ref.dtype)
    for j in range(hp):
        s_scr[j] = states[j]
        sfin_ref[j] = states[j]


def _gdn_seq(proj, conv_w, a_log, dt_bias, norm_g, layer, b, t):
    c = math.gcd(t, GDN_CHUNK)
    nc = math.gcd(t // c, GDN_SEQ_CHUNKS)
    hp = GDN_SEQ_HEADS
    assert GDN_HEADS % hp == 0
    tb = c * nc
    nt = t // tb
    wide = hp * LANES
    hq, hk, hv, hz = PK_GQ // wide, PK_GK // wide, PK_GV // wide, PK_GZ // wide
    smem = pl.BlockSpec(memory_space=pltpu.SMEM)
    cblk = lambda off: pl.BlockSpec((tb, wide), lambda bi, h, i: (bi * nt + i, off + h))
    wblk = lambda off: pl.BlockSpec((None, GDN_CONV, wide), lambda bi, h, i: (layer, 0, off + h))
    return pl.pallas_call(
        functools.partial(_gdn_seq_kernel, c=c, nc=nc, hp=hp),
        grid=(b, GDN_HEADS // hp, nt),
        in_specs=[smem, smem, cblk(hq), cblk(hk), cblk(hv), cblk(hz),
                  pl.BlockSpec((tb, LANES), lambda bi, h, i: (bi * nt + i, PK_SM // LANES)),
                  wblk(0), wblk(GDN_HEADS // hp), wblk(2 * GDN_HEADS // hp),
                  pl.BlockSpec((None, 1, LANES), lambda bi, h, i: (layer, 0, 0))],
        out_specs=[cblk(0), pl.BlockSpec((None, hp, GDN_DK, GDN_DV), lambda bi, h, i: (bi, h, 0, 0))],
        out_shape=[jax.ShapeDtypeStruct((b * t, GDN_W), BF16),
                   jax.ShapeDtypeStruct((b, GDN_HEADS, GDN_DK, GDN_DV), F32)],
        scratch_shapes=[pltpu.VMEM((tb + SUBLANES, wide), F32)] * 3 + [pltpu.VMEM((hp, GDN_DK, GDN_DV), F32)],
        compiler_params=_cparams(("arbitrary", "arbitrary", "arbitrary")),
        name="gdn_seq",
    )(a_log[layer], dt_bias[layer], proj, proj, proj, proj, proj, conv_w, conv_w, conv_w,
      norm_g.reshape(norm_g.shape[0], 1, LANES))


def _gdn_step_kernel(alog_ref, dtb_ref, x_ref, z_ref, sm_ref, w_ref, ng_ref, hist_ref, s0_ref,
                     o_ref, sfin_ref, cs, *, bb, t):
    lo = SUBLANES - (GDN_CONV - 1)
    cs[:, lo:SUBLANES, :] = hist_ref[...]
    cs[:, SUBLANES:SUBLANES + t, :] = x_ref[...].reshape(bb, t, 3 * GDN_W)
    y = cs[:, lo:lo + t, :] * w_ref[0:1, :]
    for j in range(1, GDN_CONV):
        y = y + cs[:, lo + j:lo + j + t, :] * w_ref[j:j + 1, :]
    y = _silu(y)
    sm = sm_ref[...].reshape(bb, t, LANES)
    ng = ng_ref[...]
    for h in range(GDN_HEADS):
        hs = slice(h * GDN_DK, (h + 1) * GDN_DK)
        q = _l2norm(y[:, :, hs]) * (GDN_DK ** -0.5)
        k = _l2norm(y[:, :, GDN_W + h * GDN_DK:GDN_W + (h + 1) * GDN_DK])
        v = y[:, :, 2 * GDN_W + h * GDN_DV:2 * GDN_W + (h + 1) * GDN_DV]
        beta, g = _gdn_gates(sm[:, :, SM_BETA + h:SM_BETA + h + 1], sm[:, :, SM_A + h:SM_A + h + 1],
                             alog_ref[h], dtb_ref[h])
        u, w, qk, qg, kd, gl = _gdn_prep(q, k, v, beta, g, t)
        s = s0_ref[:, h]
        vnew = u - _bmm(w, s)
        o = _bmm(qg, s) + _bmm(qk, vnew)
        sfin_ref[:, h] = s * gl + _bmm_tn(kd, vnew)
        o = _rms(o, ng) * _silu(z_ref[:, hs].reshape(bb, t, GDN_DV))
        o_ref[:, hs] = o.reshape(bb * t, GDN_DV).astype(o_ref.dtype)


def _gdn_step(proj, conv_buf, s0, conv_w, a_log, dt_bias, norm_g, layer, b, t):
    assert t % SUBLANES == 0 and GDN_CHUNK % t == 0 and t >= GDN_CONV - 1
    bb = math.gcd(b, 8)
    rows = bb * t
    smem = pl.BlockSpec(memory_space=pltpu.SMEM)
    return pl.pallas_call(
        functools.partial(_gdn_step_kernel, bb=bb, t=t),
        grid=(b // bb,),
        in_specs=[smem, smem,
                  pl.BlockSpec((rows, 3 * GDN_W), lambda bi: (bi, PK_GQ // (3 * GDN_W))),
                  pl.BlockSpec((rows, GDN_W), lambda bi: (bi, PK_GZ // GDN_W)),
                  pl.BlockSpec((rows, LANES), lambda bi: (bi, PK_SM // LANES)),
                  pl.BlockSpec((None, GDN_CONV, 3 * GDN_W), lambda bi: (layer, 0, 0)),
                  pl.BlockSpec((None, 1, LANES), lambda bi: (layer, 0, 0)),
                  pl.BlockSpec((None, bb, GDN_CONV - 1, 3 * GDN_W), lambda bi: (layer, bi, 0, 0)),
                  pl.BlockSpec((None, bb, GDN_HEADS, GDN_DK, GDN_DV), lambda bi: (layer, bi, 0, 0, 0))],
        out_specs=[pl.BlockSpec((rows, GDN_W), lambda bi: (bi, 0)),
                   pl.BlockSpec((bb, GDN_HEADS, GDN_DK, GDN_DV), lambda bi: (bi, 0, 0, 0))],
        out_shape=[jax.ShapeDtypeStruct((b * t, GDN_W), BF16),
                   jax.ShapeDtypeStruct((b, GDN_HEADS, GDN_DK, GDN_DV), F32)],
        scratch_shapes=[pltpu.VMEM((bb, t + SUBLANES, 3 * GDN_W), F32)],
        compiler_params=_cparams(("arbitrary",)),
        name="gdn_step",
    )(a_log[layer], dt_bias[layer], proj, proj, proj, conv_w, norm_g.reshape(norm_g.shape[0], 1, LANES),
      conv_buf, s0)


def _ssd_kernel(*refs, c, nc, bb, zero_init):
    if zero_init:
        (z_ref, x_ref, bc_ref, sm_ref, wx_ref, wbc_ref, bx_ref, bbc_ref, dtb_ref, alog_ref, dsk_ref, ng_ref,
         y_ref, hfin_ref, xs, xbc, h_scr) = refs
        cx_ref = cbc_ref = h0_ref = None
    else:
        (z_ref, x_ref, bc_ref, sm_ref, wx_ref, wbc_ref, bx_ref, bbc_ref, dtb_ref, alog_ref, dsk_ref, ng_ref,
         cx_ref, cbc_ref, h0_ref, y_ref, hfin_ref, xs, xbc, h_scr) = refs
    tb = c * nc
    first = pl.program_id(1) == 0
    lo = SUBLANES - (SSM_CONV - 1)

    @pl.when(first)
    def _():
        if zero_init:
            h_scr[...] = jnp.zeros_like(h_scr)
        else:
            h_scr[...] = h0_ref[...]

    def conv(cs, new_ref, hist_ref, w_ref, b_ref):
        width = cs.shape[2]

        @pl.when(first)
        def _():
            if hist_ref is None:
                cs[:, lo:SUBLANES, :] = jnp.zeros((bb, SSM_CONV - 1, width), F32)
            else:
                cs[:, lo:SUBLANES, :] = hist_ref[...]

        @pl.when(jnp.logical_not(first))
        def _():
            cs[:, lo:SUBLANES, :] = cs[:, tb + lo:tb + SUBLANES, :]

        cs[:, SUBLANES:SUBLANES + tb, :] = new_ref[...].reshape(bb, tb, width)
        y = cs[:, lo:lo + tb, :] * w_ref[0:1, :]
        for j in range(1, SSM_CONV):
            y = y + cs[:, lo + j:lo + j + tb, :] * w_ref[j:j + 1, :]
        return _silu(y + b_ref[...])

    xv = conv(xs, x_ref, cx_ref, wx_ref, bx_ref)
    bcv = conv(xbc, bc_ref, cbc_ref, wbc_ref, bbc_ref)
    gs = SSM_GROUPS * SSM_STATE

    dt_all = _softplus(sm_ref[...] + dtb_ref[...]).reshape(bb, tb, LANES)
    da_all = dt_all * (-jnp.exp(alog_ref[...]))
    _, causal, _, _ = _tri_masks(c)
    tril_f = causal.astype(F32)
    rep = SSM_HEADS // SSM_GROUPS
    gw = SSM_W // SSM_GROUPS
    ng = ng_ref[...]
    for bi in range(bb):
        for ci in range(nc):
            sl = slice(ci * c, (ci + 1) * c)
            rows = slice(bi * tb + ci * c, bi * tb + (ci + 1) * c)
            cg_all = _mm_hi(tril_f, da_all[bi, sl])
            cg_t = cg_all.T
            ys = []
            for grp in range(SSM_GROUPS):
                bm = bcv[bi, sl, grp * SSM_STATE:(grp + 1) * SSM_STATE]
                cm = bcv[bi, sl, gs + grp * SSM_STATE:gs + (grp + 1) * SSM_STATE]
                cb = _mm_nt(cm, bm)
                for hh in range(rep):
                    hd = grp * rep + hh
                    ln = SM_DT + hd
                    cg_col = cg_all[:, ln:ln + 1]
                    cg_row = cg_t[ln:ln + 1, :]
                    lmat = jnp.where(causal, jnp.exp(jnp.where(causal, cg_col - cg_row, 0.0)), 0.0)
                    xh = xv[bi, sl, hd * SSM_HEADDIM:(hd + 1) * SSM_HEADDIM]
                    xdt = xh * dt_all[bi, sl, ln:ln + 1]
                    cg_last = cg_col[c - 1:c, :]
                    hst = h_scr[bi, hd]
                    ys.append(_mm(cb * lmat, xdt) + _mm_nt(cm * jnp.exp(cg_col), hst))
                    h_scr[bi, hd] = hst * jnp.exp(cg_last) + _mm(xdt.T, bm * jnp.exp(cg_last - cg_col))
            yc = jnp.concatenate(ys, axis=-1) + dsk_ref[...] * xv[bi, sl]
            yc = yc * _silu(z_ref[rows, :])
            outs = [_rms(yc[:, gi * gw:(gi + 1) * gw], ng[:, gi * gw:(gi + 1) * gw]) for gi in range(SSM_GROUPS)]
            y_ref[rows, :] = jnp.concatenate(outs, axis=-1).astype(y_ref.dtype)

    hfin_ref[...] = h_scr[...]


def _lane_row(vals, offset):
    depth, n = vals.shape
    return jnp.pad(vals.astype(F32), ((0, 0), (offset, LANES - offset - n))).reshape(depth, 1, LANES)


def _ssd(proj, conv_buf, h0, conv_w, conv_b, dt_bias, a_log, d_skip, norm_g, layer, b, t):
    zero_init = h0 is None
    c = math.gcd(t, SSM_CHUNK)
    nc = max(1, min(4, t // c))
    tb = c * nc
    nt = t // tb
    bb = math.gcd(b, 8) if nt == 1 else 1
    rb = bb * tb
    depth = conv_w.shape[0]
    row = lambda bi, i: bi * nt + i
    wblk = SSM_W
    st_dims = (SSM_HEADS, SSM_HEADDIM, SSM_STATE)
    in_specs = [pl.BlockSpec((rb, wblk), lambda bi, i: (row(bi, i), PK_SZ // wblk)),
                pl.BlockSpec((rb, wblk), lambda bi, i: (row(bi, i), PK_SX // wblk)),
                pl.BlockSpec((rb, wblk), lambda bi, i: (row(bi, i), PK_SBC // wblk)),
                pl.BlockSpec((rb, LANES), lambda bi, i: (row(bi, i), PK_SM // LANES)),
                pl.BlockSpec((None, SSM_CONV, wblk), lambda bi, i: (layer, 0, 0)),
                pl.BlockSpec((None, SSM_CONV, wblk), lambda bi, i: (layer, 0, 1)),
                pl.BlockSpec((None, 1, wblk), lambda bi, i: (layer, 0, 0)),
                pl.BlockSpec((None, 1, wblk), lambda bi, i: (layer, 0, 1)),
                pl.BlockSpec((None, 1, LANES), lambda bi, i: (layer, 0, 0)),
                pl.BlockSpec((None, 1, LANES), lambda bi, i: (layer, 0, 0)),
                pl.BlockSpec((None, 1, SSM_W), lambda bi, i: (layer, 0, 0)),
                pl.BlockSpec((None, 1, SSM_W), lambda bi, i: (layer, 0, 0))]
    args = [proj, proj, proj, proj, conv_w, conv_w, conv_b.reshape(depth, 1, -1), conv_b.reshape(depth, 1, -1),
            _lane_row(dt_bias, SM_DT), _lane_row(a_log, SM_DT),
            jnp.repeat(d_skip.astype(F32), SSM_HEADDIM, axis=-1).reshape(depth, 1, SSM_W),
            norm_g.reshape(depth, 1, SSM_W)]
    if not zero_init:
        in_specs += [pl.BlockSpec((None, bb, SSM_CONV - 1, wblk), lambda bi, i: (layer, bi, 0, 0)),
                     pl.BlockSpec((None, bb, SSM_CONV - 1, wblk), lambda bi, i: (layer, bi, 0, 1)),
                     pl.BlockSpec((None, bb) + st_dims, lambda bi, i: (layer, bi, 0, 0, 0))]
        args += [conv_buf, conv_buf, jnp.swapaxes(h0, -1, -2)]
    y, h_fin_t = pl.pallas_call(
        functools.partial(_ssd_kernel, c=c, nc=nc, bb=bb, zero_init=zero_init),
        grid=(b // bb, nt),
        in_specs=in_specs,
        out_specs=[pl.BlockSpec((rb, SSM_W), lambda bi, i: (row(bi, i), 0)),
                   pl.BlockSpec((bb,) + st_dims, lambda bi, i: (bi, 0, 0, 0))],
        out_shape=[jax.ShapeDtypeStruct((b * t, SSM_W), BF16), jax.ShapeDtypeStruct((b,) + st_dims, F32)],
        scratch_shapes=[pltpu.VMEM((bb, tb + SUBLANES, wblk), F32)] * 2 + [pltpu.VMEM((bb,) + st_dims, F32)],
        compiler_params=_cparams(("arbitrary", "arbitrary")),
        name="ssd",
    )(*args)
    return y, jnp.swapaxes(h_fin_t, -1, -2)


def _rope_tables(pos0, t):
    half = SWA_HD // 2
    inv = ROPE_THETA ** (-jnp.arange(half, dtype=F32) / half)
    ang = (pos0 + jnp.arange(t, dtype=jnp.int32)).astype(F32)[:, None] * inv[None, :]
    cos, sin = jnp.cos(ang), jnp.sin(ang)
    return jnp.concatenate([cos, cos], axis=-1), jnp.concatenate([-sin, sin], axis=-1)


def _rope(x, cosf, sinf):
    return x * cosf + pltpu.roll(x, SWA_HD // 2, axis=1) * sinf


def _swa_prompt_kernel(q_ref, k_ref, v_ref, cos_ref, sin_ref, k_prev, v_prev, o_ref, kr_ref, vo_ref,
                       qs, p_num, p_den, p_mx, *, t):
    del k_prev, v_prev
    qb = LANES
    cosf, sinf = cos_ref[...], sin_ref[...]
    qs[...] = _rope(q_ref[...], cosf, sinf) * (SWA_HD ** -0.5)
    kr_ref[...] = _rope(k_ref[...], cosf, sinf)
    vo_ref[...] = v_ref[...]

    m_i = lax.broadcasted_iota(jnp.int32, (qb, 2 * qb), 0)
    n_i = lax.broadcasted_iota(jnp.int32, (qb, 2 * qb), 1)
    mask_two = (n_i >= m_i) & (n_i <= m_i + qb)
    mask_one = (lax.broadcasted_iota(jnp.int32, (qb, qb), 1)
                <= lax.broadcasted_iota(jnp.int32, (qb, qb), 0))

    def block(br, q0, k0, nk, dil):
        if dil == 1:
            qi, ki = pl.ds(q0, qb), pl.ds(k0, nk)
        else:
            qi, ki = pl.ds(q0, qb, stride=dil), pl.ds(k0, nk, stride=dil)
        sc = _mm_nt(qs[qi, :], kr_ref[ki, :])
        sc = jnp.where(mask_one if nk == qb else mask_two, sc, NEG_BIG)
        mx = jnp.max(sc, axis=-1, keepdims=True)
        p = jnp.exp(sc - mx)
        p_num[br, qi, :] = _mm(p, v_ref[ki, :])
        p_den[br, qi, :] = jnp.broadcast_to(jnp.sum(p, axis=-1, keepdims=True), (qb, LANES))
        p_mx[br, qi, :] = jnp.broadcast_to(mx, (qb, LANES))

    for br, (window, dil) in enumerate(DILATIONS):
        assert window == qb * dil and t % (qb * dil) == 0
        nblk = t // (qb * dil)

        def residue(r, carry, br=br, dil=dil, nblk=nblk):
            block(br, r, r, qb, dil)
            if nblk > 1:
                def later(j, cc):
                    block(br, r + dil * qb * j, r + dil * qb * (j - 1), 2 * qb, dil)
                    return cc
                lax.fori_loop(1, nblk, later, 0, unroll=True)
            return carry

        if dil == 1:
            residue(0, 0)
        else:
            lax.fori_loop(0, dil, residue, 0, unroll=16 if nblk == 1 else 4)

    rows = 2 * qb

    def merge(i, carry):
        sl = pl.ds(pl.multiple_of(i * rows, rows), rows)
        mxs = [p_mx[br, sl, :] for br in range(len(DILATIONS))]
        mx = functools.reduce(jnp.maximum, mxs)
        wts = [jnp.exp(m - mx) for m in mxs]
        num = sum(p_num[br, sl, :] * wts[br] for br in range(len(DILATIONS)))
        den = sum(p_den[br, sl, :] * wts[br] for br in range(len(DILATIONS)))
        o_ref[sl, :] = (num / den).astype(o_ref.dtype)
        return carry

    lax.fori_loop(0, t // rows, merge, 0)


def _swa_prompt(proj, layer, k_all, v_all, b, t):
    cosf, sinf = _rope_tables(0, t)
    hq, hk, hv = PK_AQ // LANES, PK_AK // LANES, PK_AV // LANES
    kv_spec = pl.BlockSpec((None, None, None, t, SWA_HD), lambda bi, h: (layer, bi, h, 0, 0))
    hbm = pl.BlockSpec(memory_space=pl.ANY)
    return pl.pallas_call(
        functools.partial(_swa_prompt_kernel, t=t),
        grid=(b, SWA_HEADS),
        in_specs=[pl.BlockSpec((t, LANES), lambda bi, h: (bi, hq + h)),
                  pl.BlockSpec((t, LANES), lambda bi, h: (bi, hk + h)),
                  pl.BlockSpec((t, LANES), lambda bi, h: (bi, hv + h)),
                  pl.BlockSpec((t, LANES), lambda bi, h: (0, 0)),
                  pl.BlockSpec((t, LANES), lambda bi, h: (0, 0)),
                  hbm, hbm],
        out_specs=[pl.BlockSpec((t, LANES), lambda bi, h: (bi, h)), kv_spec, kv_spec],
        out_shape=[jax.ShapeDtypeStruct((b * t, SWA_W), BF16),
                   jax.ShapeDtypeStruct(k_all.shape, F32), jax.ShapeDtypeStruct(v_all.shape, F32)],
        input_output_aliases={5: 1, 6: 2},
        scratch_shapes=[pltpu.VMEM((t, LANES), F32)] + [pltpu.VMEM((len(DILATIONS), t, LANES), F32)] * 3,
        compiler_params=_cparams(("arbitrary", "arbitrary")),
        name="swa_prompt",
    )(proj, proj, proj, cosf, sinf, k_all, v_all)


def _swa_sample_kernel(q_ref, k_ref, v_ref, cos_ref, sin_ref, kf, vf, kn, vn, o_ref, kr_ref, *, t, past):
    (w1, d1), (w2, d2), (w3, d3) = DILATIONS
    n3 = kf.shape[1] * kf.shape[2]
    n2 = kn.shape[1] * kn.shape[2]
    n_l = kn.shape[1] * d3
    tail0 = n3 - kn.shape[1] * t
    assert d3 == 2 * t and w3 == past and n_l == w2 and w1 <= n_l and tail0 % LANES == 0
    assert all(d & (d - 1) == 0 for d in (t, d1, d2, d3))
    cosf, sinf = cos_ref[...], sin_ref[...]

    def masks(n, abs_of_col, branches):
        m_i = lax.broadcasted_iota(jnp.int32, (t, n), 0)
        dist = past + m_i - abs_of_col(lax.broadcasted_iota(jnp.int32, (t, n), 1))
        return [(dist >= 0) & (dist <= w) & ((dist & (d - 1)) == 0) for w, d in branches]

    shift = t.bit_length() - 1
    near_br, far_br = DILATIONS[:2], DILATIONS[2:]
    (mask3f,) = masks(n3, lambda c: d3 * (c >> shift) + (c & (t - 1)), far_br)
    mask1t, mask2t = masks(n3 - tail0, lambda c: d3 * ((c + tail0) >> shift) + (c & (t - 1)), near_br)
    mask1c, mask2c = masks(n2, lambda c: past - n_l + d3 * (c >> shift) + t + (c & (t - 1)), near_br)
    mask1n, mask2n, mask3n = masks(t, lambda c: past + c, DILATIONS)

    for h in range(SWA_HEADS):
        hs = slice(h * SWA_HD, (h + 1) * SWA_HD)
        qh = _rope(q_ref[:, hs], cosf, sinf) * (SWA_HD ** -0.5)
        k_new = _rope(k_ref[:, hs], cosf, sinf)
        kr_ref[:, hs] = k_new
        v_new = v_ref[:, hs]
        k3 = kf[h].reshape(n3, SWA_HD)
        v3 = vf[h].reshape(n3, SWA_HD)
        k2 = kn[h].reshape(n2, SWA_HD)
        v2 = vn[h].reshape(n2, SWA_HD)
        s3 = _mm_nt(qh, k3)
        s_t = s3[:, tail0:]
        s_c = _mm_nt(qh, k2)
        s_n = _mm_nt(qh, k_new)

        def branch(parts):
            mx = None
            for sc, mask in parts:
                cur = jnp.max(jnp.where(mask, sc, NEG_BIG), axis=-1, keepdims=True)
                mx = cur if mx is None else jnp.maximum(mx, cur)
            ps = [jnp.where(mask, jnp.exp(jnp.where(mask, sc, NEG_BIG) - mx), 0.0) for sc, mask in parts]
            den = sum(jnp.sum(p, axis=-1, keepdims=True) for p in ps)
            return mx, ps, den

        mx1, (p1t, p1c, p1n), den1 = branch([(s_t, mask1t), (s_c, mask1c), (s_n, mask1n)])
        mx2, (p2t, p2c, p2n), den2 = branch([(s_t, mask2t), (s_c, mask2c), (s_n, mask2n)])
        mx3, (p3f, p3n), den3 = branch([(s3, mask3f), (s_n, mask3n)])
        mx = jnp.maximum(jnp.maximum(mx1, mx2), mx3)
        wt1, wt2, wt3 = jnp.exp(mx1 - mx), jnp.exp(mx2 - mx), jnp.exp(mx3 - mx)
        num = (_mm(p3f * wt3, v3) + _mm(p1t * wt1 + p2t * wt2, v3[tail0:])
               + _mm(p1c * wt1 + p2c * wt2, v2) + _mm(p1n * wt1 + p2n * wt2 + p3n * wt3, v_new))
        den = den1 * wt1 + den2 * wt2 + den3 * wt3
        o_ref[:, hs] = (num / den).astype(o_ref.dtype)


def _swa_sample(proj, cache_k, cache_v, layer, b, t, past):
    depth, _, l_cache, nh, hd = cache_k.shape
    assert l_cache == past and nh * hd == SWA_W
    (_, _), (w2, _), (_, d3) = DILATIONS
    cosf, sinf = _rope_tables(past, t)
    assert l_cache % d3 == 0 and l_cache % w2 == 0 and t % SUBLANES == 0
    ckh = jnp.swapaxes(cache_k, 2, 3)
    cvh = jnp.swapaxes(cache_v, 2, 3)
    assert d3 == 2 * t
    far = pl.BlockSpec((None, None, nh, l_cache // d3, t, hd), lambda bi: (layer, bi, 0, 0, 0, 0))
    near = pl.BlockSpec((None, None, nh, w2 // d3, t, hd), lambda bi: (layer, bi, 0, l_cache // w2 - 1, 1, 0))
    split = lambda c: c.reshape(depth, b, nh, l_cache // d3, d3, hd)
    return pl.pallas_call(
        functools.partial(_swa_sample_kernel, t=t, past=past),
        grid=(b,),
        in_specs=[pl.BlockSpec((t, SWA_W), lambda bi: (bi, PK_AQ // SWA_W)),
                  pl.BlockSpec((t, SWA_W), lambda bi: (bi, PK_AK // SWA_W)),
                  pl.BlockSpec((t, SWA_W), lambda bi: (bi, PK_AV // SWA_W)),
                  pl.BlockSpec((t, LANES), lambda bi: (0, 0)),
                  pl.BlockSpec((t, LANES), lambda bi: (0, 0)),
                  far, far, near, near],
        out_specs=[pl.BlockSpec((t, SWA_W), lambda bi: (bi, 0)),
                   pl.BlockSpec((t, SWA_W), lambda bi: (bi, 0))],
        out_shape=[jax.ShapeDtypeStruct((b * t, SWA_W), BF16),
                   jax.ShapeDtypeStruct((b * t, SWA_W), F32)],
        compiler_params=_cparams(("arbitrary",)),
        name="swa_sample",
    )(proj, proj, proj, cosf, sinf, split(ckh), split(cvh), split(ckh), split(cvh))


def _outproj_kernel(oa_ref, ob_ref, oc_ref, w_ref, x_ref, gate_ref, gpost_ref, gpre_ref, sh_ref, sc_ref,
                    xo_ref, h2_ref):
    mix = (jnp.dot(oa_ref[...], w_ref[0:GDN_W, :], preferred_element_type=F32)
           + jnp.dot(ob_ref[...], w_ref[GDN_W:GDN_W + SSM_W, :], preferred_element_type=F32)
           + jnp.dot(oc_ref[...], w_ref[GDN_W + SSM_W:, :], preferred_element_type=F32))
    y = _rms(mix, gpost_ref[...])
    x = x_ref[...] + gate_ref[...] * y.reshape(x_ref.shape)
    xo_ref[...] = x
    h2 = _rms(x, gpre_ref[...]) * (1.0 + sc_ref[...]) + sh_ref[...]
    h2_ref[...] = h2.reshape(h2_ref.shape).astype(h2_ref.dtype)


def _outproj(oa, ob, oc, w, x, mod, gpost, gpre, layer, bb, tt):
    b, t, d = x.shape
    nt = t // tt
    rows = bb * tt
    rmap = lambda bi, ti: (bi * nt + ti, 0)
    mspec = lambda k: pl.BlockSpec((None, bb, 1, d), lambda bi, ti: (layer, bi, 0, k))
    gspec = pl.BlockSpec((None, 1, d), lambda bi, ti: (layer, 0, 0))
    return pl.pallas_call(
        _outproj_kernel,
        grid=(b // bb, nt),
        in_specs=[pl.BlockSpec((rows, GDN_W), rmap), pl.BlockSpec((rows, SSM_W), rmap),
                  pl.BlockSpec((rows, SWA_W), rmap),
                  pl.BlockSpec((None, d, d), lambda bi, ti: (layer, 0, 0)),
                  pl.BlockSpec((bb, tt, d), lambda bi, ti: (bi, ti, 0)),
                  mspec(2), gspec, gspec, mspec(3), mspec(4)],
        out_specs=[pl.BlockSpec((bb, tt, d), lambda bi, ti: (bi, ti, 0)),
                   pl.BlockSpec((rows, d), rmap)],
        out_shape=[jax.ShapeDtypeStruct((b, t, d), F32), jax.ShapeDtypeStruct((b * t, d), BF16)],
        compiler_params=_cparams(("arbitrary", "arbitrary")),
        name="outproj",
    )(oa, ob, oc, w, x, mod, gpost, gpre, mod, mod)


def _ffn_up_kernel(*refs, tt, zero_init):
    if zero_init:
        (h_ref, wg_ref, wu_ref, cwg_ref, cwu_ref, cbg_ref, cbu_ref, act_ref, nsg_ref, nsu_ref, csg, csu) = refs
        stg_ref = stu_ref = None
    else:
        (h_ref, wg_ref, wu_ref, cwg_ref, cwu_ref, cbg_ref, cbu_ref, stg_ref, stu_ref,
         act_ref, nsg_ref, nsu_ref, csg, csu) = refs
    first = pl.program_id(2) == 0
    lo = SUBLANES - (FFN_CONV - 1)
    bb, _, tn = csg.shape

    for st_ref, cs in ((stg_ref, csg), (stu_ref, csu)):
        @pl.when(first)
        def _(st_ref=st_ref, cs=cs):
            if st_ref is None:
                cs[:, lo:SUBLANES, :] = jnp.zeros((bb, FFN_CONV - 1, tn), F32)
            else:
                cs[:, lo:SUBLANES, :] = st_ref[...]

        @pl.when(jnp.logical_not(first))
        def _(cs=cs):
            cs[:, lo:SUBLANES, :] = cs[:, tt + lo:tt + SUBLANES, :]

    h = h_ref[...]
    sub = 2 * LANES

    def half(w_ref, cw_ref, cb_ref, ns_ref, cs, cols):
        up = jnp.dot(h, w_ref[:, cols], preferred_element_type=F32)
        cs[:, SUBLANES:SUBLANES + tt, cols] = up.reshape(bb, tt, up.shape[-1])
        y = cs[:, lo:lo + tt, cols] * cw_ref[0:1, cols]
        for j in range(1, FFN_CONV):
            y = y + cs[:, lo + j:lo + j + tt, cols] * cw_ref[j:j + 1, cols]
        ns_ref[:, :, cols] = cs[:, tt + lo:tt + SUBLANES, cols]
        return y + cb_ref[:, cols]

    for c0 in range(0, tn, sub):
        cols = slice(c0, min(c0 + sub, tn))
        yg = half(wg_ref, cwg_ref, cbg_ref, nsg_ref, csg, cols)
        yu = half(wu_ref, cwu_ref, cbu_ref, nsu_ref, csu, cols)
        act = _silu(yg) * yu
        act_ref[:, cols] = act.reshape(bb * tt, act.shape[-1]).astype(act_ref.dtype)


def _ffn_up(h2, state, w_up, conv_w, conv_b, layer, b, t, bb, tt):
    zero_init = state is None
    d = h2.shape[1]
    tn = 1408 if (bb == 1 and tt <= 1024) else 512
    nh = D_FF // tn
    nt = t // tt
    rows = bb * tt
    depth = conv_w.shape[0]
    cb = conv_b.reshape(depth, 1, -1)
    in_specs = [pl.BlockSpec((rows, d), lambda bi, j, ti: (bi * nt + ti, 0)),
                pl.BlockSpec((None, d, tn), lambda bi, j, ti: (layer, 0, j)),
                pl.BlockSpec((None, d, tn), lambda bi, j, ti: (layer, 0, nh + j)),
                pl.BlockSpec((None, FFN_CONV, tn), lambda bi, j, ti: (layer, 0, j)),
                pl.BlockSpec((None, FFN_CONV, tn), lambda bi, j, ti: (layer, 0, nh + j)),
                pl.BlockSpec((None, 1, tn), lambda bi, j, ti: (layer, 0, j)),
                pl.BlockSpec((None, 1, tn), lambda bi, j, ti: (layer, 0, nh + j))]
    args = [h2, w_up, w_up, conv_w, conv_w, cb, cb]
    if not zero_init:
        in_specs += [pl.BlockSpec((None, bb, FFN_CONV - 1, tn), lambda bi, j, ti: (layer, bi, 0, j)),
                     pl.BlockSpec((None, bb, FFN_CONV - 1, tn), lambda bi, j, ti: (layer, bi, 0, nh + j))]
        args += [state, state]
    ns_spec = pl.BlockSpec((bb, FFN_CONV - 1, tn), lambda bi, j, ti: (bi, 0, j))
    act, nsg, nsu = pl.pallas_call(
        functools.partial(_ffn_up_kernel, tt=tt, zero_init=zero_init),
        grid=(b // bb, nh, nt),
        in_specs=in_specs,
        out_specs=[pl.BlockSpec((rows, tn), lambda bi, j, ti: (bi * nt + ti, j)), ns_spec, ns_spec],
        out_shape=[jax.ShapeDtypeStruct((b * t, D_FF), BF16),
                   jax.ShapeDtypeStruct((b, FFN_CONV - 1, D_FF), F32),
                   jax.ShapeDtypeStruct((b, FFN_CONV - 1, D_FF), F32)],
        scratch_shapes=[pltpu.VMEM((bb, tt + SUBLANES, tn), F32)] * 2,
        compiler_params=_cparams(("arbitrary", "arbitrary", "arbitrary")),
        name="ffn_up",
    )(*args)
    return act, jnp.concatenate([nsg, nsu], axis=-1)


def _ffn_down_kernel(a_ref, w_ref, x_ref, gate_ref, g_ref, o_ref):
    k = pl.program_id(2)
    part = lambda: jnp.dot(a_ref[...], w_ref[...], preferred_element_type=F32).reshape(o_ref.shape)

    @pl.when(k == 0)
    def _():
        o_ref[...] = part()

    @pl.when(k > 0)
    def _():
        o_ref[...] += part()

    @pl.when(k == pl.num_programs(2) - 1)
    def _():
        o_ref[...] = x_ref[...] + gate_ref[...] * _rms(o_ref[...], g_ref[...])


def _ffn_down(act, w_down, x, mod, g, layer, bb, tt):
    b, t, d = x.shape
    nt = t // tt
    rows = bb * tt
    tk = 512
    return pl.pallas_call(
        _ffn_down_kernel,
        grid=(b // bb, nt, D_FF // tk),
        in_specs=[pl.BlockSpec((rows, tk), lambda bi, ti, k: (bi * nt + ti, k)),
                  pl.BlockSpec((None, tk, d), lambda bi, ti, k: (layer, k, 0)),
                  pl.BlockSpec((bb, tt, d), lambda bi, ti, k: (bi, ti, 0)),
                  pl.BlockSpec((None, bb, 1, d), lambda bi, ti, k: (layer, bi, 0, 5)),
                  pl.BlockSpec((None, 1, d), lambda bi, ti, k: (layer, 0, 0))],
        out_specs=pl.BlockSpec((bb, tt, d), lambda bi, ti, k: (bi, ti, 0)),
        out_shape=jax.ShapeDtypeStruct((b, t, d), F32),
        compiler_params=_cparams(("arbitrary", "arbitrary", "arbitrary")),
        name="ffn_down",
    )(act, w_down, x, mod, g)


def _pack_w_in(w_in):
    depth, d, _ = w_in.shape
    o_gb = 4 * GDN_W
    o_sz = o_gb + 2 * GDN_HEADS
    o_dt = o_sz + SSM_W + SSM_CONV_CH
    o_aq = o_dt + SSM_HEADS
    pad = jnp.zeros((depth, d, LANES - 2 * GDN_HEADS - SSM_HEADS), w_in.dtype)
    packed = jnp.concatenate([w_in[..., :o_gb], w_in[..., o_sz:o_dt], w_in[..., o_aq:],
                              w_in[..., o_gb:o_sz], w_in[..., o_dt:o_aq], pad], axis=-1)
    assert packed.shape[-1] == PK_COLS
    return packed.astype(BF16)


def _layer(x, mod, layer, states, kv_all, wts, past, bb, tt):
    b, t, d = x.shape
    if states is None:
        gdn_conv = gdn_s = ssm_conv = ssm_h = cache_k = cache_v = ffn_conv = None
    else:
        gdn_conv, gdn_s, ssm_conv, ssm_h, cache_k, cache_v, ffn_conv = states
    tt2 = 2 * tt if (bb == 1 and t % (2 * tt) == 0) else tt
    proj = _inproj(x, mod, wts["g_pre_mix"], wts["w_in"], layer, bb, tt2)
    gdn_w = (wts["gdn_conv_w"], wts["gdn_a_log"], wts["gdn_dt_bias"], wts["gdn_norm_g"])
    if states is None:
        o_a, new_gdn_s = _gdn_seq(proj, *gdn_w, layer, b, t)
    else:
        o_a, new_gdn_s = _gdn_step(proj, gdn_conv, gdn_s, *gdn_w, layer, b, t)
    o_b, new_ssm_h = _ssd(proj, ssm_conv, ssm_h, wts["ssm_conv_w"], wts["ssm_conv_b"], wts["ssm_dt_bias"],
                          wts["ssm_a_log"], wts["ssm_d"], wts["ssm_norm_g"], layer, b, t)
    if states is None:
        o_c, *kv_all = _swa_prompt(proj, layer, *kv_all, b, t)
    else:
        o_c, k_rot = _swa_sample(proj, cache_k, cache_v, layer, b, t, past)
    x_mid, h2 = _outproj(o_a, o_b, o_c, wts["w_out"], x, mod, wts["g_post_mix"], wts["g_pre_ffn"], layer, bb, tt)
    act, new_ffn_conv = _ffn_up(h2, ffn_conv, wts["w_up"], wts["ffn_conv_w"], wts["ffn_conv_b"], layer, b, t, bb, tt2)
    x_out = _ffn_down(act, wts["w_down"], x_mid, mod, wts["g_post_ffn"], layer, bb, tt2)

    proj3 = proj.reshape(b, t, PK_COLS)
    new_gdn_conv = proj3[:, t - (GDN_CONV - 1):, PK_GQ:PK_GQ + 3 * GDN_W]
    new_ssm_conv = proj3[:, t - (SSM_CONV - 1):, PK_SX:PK_SX + SSM_CONV_CH]
    outs = [new_gdn_conv, new_gdn_s, new_ssm_conv, new_ssm_h, new_ffn_conv]
    if states is not None:
        outs += [k_rot.reshape(b, t, SWA_HEADS, SWA_HD), proj3[:, :, PK_AV:PK_AV + SWA_W].reshape(b, t, SWA_HEADS, SWA_HD)]
    return x_out, outs, kv_all


def kernel(x_prompt, x_sample, c_prompt, c_sample, state_gdn_conv, state_gdn, state_ssm_conv, state_ssm, cache_k, cache_v, state_ffn_conv, w_ada, b_ada, g_pre_mix, g_post_mix, g_pre_ffn, g_post_ffn, w_in, gdn_conv_w, gdn_a_log, gdn_dt_bias, gdn_norm_g, ssm_conv_w, ssm_conv_b, ssm_dt_bias, ssm_a_log, ssm_d, ssm_norm_g, w_out, w_up, ffn_conv_w, ffn_conv_b, w_down):
    depth = w_ada.shape[0]
    bp, tp, d = x_prompt.shape
    bs, ts, _ = x_sample.shape
    past = cache_k.shape[2]
    assert tp >= GDN_CONV and ts >= GDN_CONV and ts % SUBLANES == 0

    vec = lambda a: a.reshape(depth, 1, a.shape[-1])
    wts = dict(
        g_pre_mix=vec(g_pre_mix), g_post_mix=vec(g_post_mix), g_pre_ffn=vec(g_pre_ffn), g_post_ffn=vec(g_post_ffn),
        w_in=_pack_w_in(w_in), w_out=w_out.astype(BF16), w_up=w_up.astype(BF16), w_down=w_down.astype(BF16),
        gdn_conv_w=gdn_conv_w, gdn_a_log=gdn_a_log, gdn_dt_bias=gdn_dt_bias, gdn_norm_g=gdn_norm_g,
        ssm_conv_w=ssm_conv_w, ssm_conv_b=ssm_conv_b, ssm_dt_bias=ssm_dt_bias, ssm_a_log=ssm_a_log, ssm_d=ssm_d,
        ssm_norm_g=ssm_norm_g, ffn_conv_w=ffn_conv_w, ffn_conv_b=ffn_conv_b)

    mod = _ada(jnp.concatenate([c_prompt, c_sample], axis=0), w_ada, b_ada)
    mod_p = mod[:, :bp].reshape(depth, bp, 1, 6 * d)
    mod_s = mod[:, bp:].reshape(depth, bs, 1, 6 * d)
    s_states = (state_gdn_conv, state_gdn, state_ssm_conv, state_ssm, cache_k, cache_v, state_ffn_conv)

    tt_p = math.gcd(tp, 512)
    bb_s = math.gcd(bs, 512 // ts)
    assert tp <= W_MAX and ts <= W_MAX
    y_p, y_s = x_prompt, x_sample
    kv_p = [jnp.zeros((depth, bp, SWA_HEADS, tp, SWA_HD), F32) for _ in range(2)]
    p_rows, s_rows = [], []
    for layer in range(depth):
        y_p, outs_p, kv_p = _layer(y_p, mod_p, layer, None, kv_p, wts, 0, 1, tt_p)
        y_s, outs_s, _ = _layer(y_s, mod_s, layer, s_states, None, wts, past, bb_s, ts)
        p_rows.append(outs_p)
        s_rows.append(outs_s)
    p_gdn_conv, p_gdn, p_ssm_conv, p_ssm, p_ffn_conv = [jnp.stack(a) for a in zip(*p_rows)]
    s_gdn_conv, s_gdn, s_ssm_conv, s_ssm, s_ffn_conv, s_k, s_v = [jnp.stack(a) for a in zip(*s_rows)]
    p_k, p_v = [jnp.swapaxes(a, 2, 3) for a in kv_p]
    return (y_p, y_s, p_gdn_conv, p_gdn, p_ssm_conv, p_ssm, p_k, p_v, p_ffn_conv,
            s_gdn_conv, s_gdn, s_ssm_conv, s_ssm, s_k, s_v, s_ffn_conv)
```
